```python
import jax, jax.numpy as jnp
from jax import lax
import numpy as np

D_MODEL = 2048
BATCH = 2
SEQ = 8192
DEPTH = 1
DEC_BATCH = 4
DEC_SEQ = 8192
PAST_LEN = 128

GRID_W = 64
NA_WIDTH = D_MODEL // 2
NA_HEAD_DIM = 64
NA_HEADS = NA_WIDTH // NA_HEAD_DIM
WIN_ROWS = 8
WIN_COLS = 16
POOL_WINDOWS = (2, 4, 8, 16)
POOL_WIDTH = D_MODEL // 2
POOL_GROUP = POOL_WIDTH // len(POOL_WINDOWS)
MIX_WIDTH = NA_WIDTH + POOL_WIDTH
IN_WIDTH = 3 * NA_WIDTH + POOL_WIDTH
MEM_LEN = 256
MEM_HEADS = 4
MEM_HEAD_DIM = 128
MEM_WIDTH = MEM_HEADS * MEM_HEAD_DIM
N_EXPERTS = 16
EC_CAPACITY_FACTOR = 2
D_FF = D_MODEL
EPS = 1e-6
NEG_INF = -1e30

kernel_name = "hybrid_na_pool_ec_encoder"


def _rmsnorm(x, g):
    xf = x.astype(jnp.float32)
    y = xf * lax.rsqrt(jnp.mean(xf * xf, axis=-1, keepdims=True) + EPS)
    return (y * g.astype(jnp.float32)).astype(x.dtype)


def _neighbourhood_attention(q, k, v, rpb):
    b, t, h, dh = q.shape
    rows = t // GRID_W
    wr = min(WIN_ROWS, rows)
    qg = q.reshape(b, rows, GRID_W, h, dh).transpose(1, 0, 2, 3, 4)
    kg = k.reshape(b, rows, GRID_W, h, dh)
    vg = v.reshape(b, rows, GRID_W, h, dh)
    cols = jnp.arange(GRID_W)
    c0 = jnp.clip(cols - WIN_COLS // 2, 0, GRID_W - WIN_COLS)
    col_mask = (cols[None, :] >= c0[:, None]) & (cols[None, :] < c0[:, None] + WIN_COLS)
    dc_idx = jnp.clip(cols[None, :] - cols[:, None], -(WIN_COLS - 1), WIN_COLS - 1) + WIN_COLS - 1
    rpb_f = rpb.astype(jnp.float32)
    scale = dh ** -0.5

    def row_block(args):
        r, q_row = args
        r0 = jnp.clip(r - wr // 2, 0, rows - wr)
        k_blk = lax.dynamic_slice_in_dim(kg, r0, wr, axis=1)
        v_blk = lax.dynamic_slice_in_dim(vg, r0, wr, axis=1)
        s = jnp.einsum('bqhd,bikhd->bhqik', q_row, k_blk).astype(jnp.float32) * scale
        dr_idx = r0 - r + jnp.arange(wr) + WIN_ROWS - 1
        bias = rpb_f[:, dr_idx][:, :, dc_idx]
        s = s + bias.transpose(0, 2, 1, 3)[None]
        s = jnp.where(col_mask[None, None, :, None, :], s, NEG_INF)
        p = jax.nn.softmax(s.reshape(b, h, GRID_W, wr * GRID_W), axis=-1)
        p = p.reshape(s.shape).astype(v.dtype)
        return jnp.einsum('bhqik,bikhd->bqhd', p, v_blk)

    o = lax.map(row_block, (jnp.arange(rows), qg))
    return o.transpose(1, 0, 2, 3, 4).reshape(b, t, h, dh)


def _multiscale_pool(u, w_pool, pool_scale):
    b, t, c = u.shape
    uf = u.astype(jnp.float32)
    csum = jnp.concatenate([jnp.zeros((b, 1, c), jnp.float32), jnp.cumsum(uf, axis=1)], axis=1)
    pos = jnp.arange(t)
    outs = []
    for g, w in enumerate(POOL_WINDOWS):
        sl = slice(g * POOL_GROUP, (g + 1) * POOL_GROUP)
        lo = jnp.clip(pos - w // 2, 0, t)
        hi = jnp.clip(pos + w // 2, 0, t)
        cg = csum[:, :, sl]
        mean = (cg[:, hi] - cg[:, lo]) / (hi - lo).astype(jnp.float32)[None, :, None]
        outs.append(mean - uf[:, :, sl])
    d = jnp.stack(outs, axis=2).astype(u.dtype)
    y = jnp.einsum('btgc,gce->btge', d, w_pool)
    return y.reshape(b, t, c) * pool_scale


def _memory_attention(h, mem_n, w_q, w_k, w_v, w_o):
    b, t, _ = h.shape
    m = mem_n.shape[1]
    q = (h @ w_q).reshape(b, t, MEM_HEADS, MEM_HEAD_DIM)
    k = (mem_n @ w_k).reshape(b, m, MEM_HEADS, MEM_HEAD_DIM)
    v = (mem_n @ w_v).reshape(b, m, MEM_HEADS, MEM_HEAD_DIM)
    s = jnp.einsum('bthd,bmhd->bhtm', q, k).astype(jnp.float32) * (MEM_HEAD_DIM ** -0.5)
    p = jax.nn.softmax(s, axis=-1).astype(v.dtype)
    o = jnp.einsum('bhtm,bmhd->bthd', p, v).reshape(b, t, MEM_WIDTH)
    return o @ w_o


def _expert_choice_ffn(h, w_router, w_gate, w_up, w_down):
    b, t, d = h.shape
    n = b * t
    cap = EC_CAPACITY_FACTOR * n // N_EXPERTS
    hf = h.reshape(n, d)
    aff = jax.nn.softmax((hf @ w_router).astype(jnp.float32), axis=-1)
    gates, idx = lax.top_k(aff.T, cap)
    xs = hf[idx]
    a = jnp.einsum('ecd,edf->ecf', xs, w_gate)
    up = jnp.einsum('ecd,edf->ecf', xs, w_up)
    y = jnp.einsum('ecf,efd->ecd', jax.nn.silu(a) * up, w_down) * gates[..., None].astype(h.dtype)
    out = jnp.zeros_like(hf).at[idx.reshape(-1)].add(y.reshape(-1, d))
    return out.reshape(b, t, d)


def _trunk(x, mem, g_mix, w_in, rpb, w_pool, pool_scale, w_out, g_mq, g_mkv, w_mq, w_mk, w_mv, w_mo,
           g_ffn, w_router, w_gate, w_up, w_down, g_final):
    b, t, _ = x.shape
    for l in range(DEPTH):
        h = _rmsnorm(x, g_mix[l])
        z = h @ w_in[l]
        q = z[..., :NA_WIDTH].reshape(b, t, NA_HEADS, NA_HEAD_DIM)
        k = z[..., NA_WIDTH:2 * NA_WIDTH].reshape(b, t, NA_HEADS, NA_HEAD_DIM)
        v = z[..., 2 * NA_WIDTH:3 * NA_WIDTH].reshape(b, t, NA_HEADS, NA_HEAD_DIM)
        u = z[..., 3 * NA_WIDTH:]
        o_na = _neighbourhood_attention(q, k, v, rpb[l]).reshape(b, t, NA_WIDTH)
        o_pool = _multiscale_pool(u, w_pool[l], pool_scale[l])
        x = x + jnp.concatenate([o_na, o_pool], axis=-1) @ w_out[l]
        x = x + _memory_attention(_rmsnorm(x, g_mq[l]), _rmsnorm(mem, g_mkv[l]),
                                  w_mq[l], w_mk[l], w_mv[l], w_mo[l])
        x = x + _expert_choice_ffn(_rmsnorm(x, g_ffn[l]), w_router[l], w_gate[l], w_up[l], w_down[l])
    return _rmsnorm(x, g_final)


def setup_inputs(seed: int = 0) -> dict:
    key = jax.random.key(seed)
    ks = jax.random.split(key, 24)
    f32 = jnp.float32

    def nrm(k, shape, fan_in):
        return jax.random.normal(k, shape, f32) * (fan_in ** -0.5)

    def gain(k, shape):
        return 1.0 + 0.02 * jax.random.normal(k, shape, f32)

    return {
        "x_prompt": jax.random.normal(ks[0], (BATCH, SEQ, D_MODEL), f32),
        "x_sample": jax.random.normal(ks[1], (DEC_BATCH, DEC_SEQ, D_MODEL), f32),
        "mem_prompt": jax.random.normal(ks[2], (BATCH, MEM_LEN, D_MODEL), f32),
        "mem_sample": jax.random.normal(ks[3], (DEC_BATCH, MEM_LEN, D_MODEL), f32),
        "g_mix": gain(ks[4], (DEPTH, D_MODEL)),
        "w_in": nrm(ks[5], (DEPTH, D_MODEL, IN_WIDTH), D_MODEL),
        "rpb": 0.1 * jax.random.normal(ks[6], (DEPTH, NA_HEADS, 2 * WIN_ROWS - 1, 2 * WIN_COLS - 1), f32),
        "w_pool": nrm(ks[7], (DEPTH, len(POOL_WINDOWS), POOL_GROUP, POOL_GROUP), POOL_GROUP),
        "pool_scale": gain(ks[8], (DEPTH, POOL_WIDTH)),
        "w_out": nrm(ks[9], (DEPTH, MIX_WIDTH, D_MODEL), MIX_WIDTH),
        "g_mq": gain(ks[10], (DEPTH, D_MODEL)),
        "g_mkv": gain(ks[11], (DEPTH, D_MODEL)),
        "w_mq": nrm(ks[12], (DEPTH, D_MODEL, MEM_WIDTH), D_MODEL),
        "w_mk": nrm(ks[13], (DEPTH, D_MODEL, MEM_WIDTH), D_MODEL),
        "w_mv": nrm(ks[14], (DEPTH, D_MODEL, MEM_WIDTH), D_MODEL),
        "w_mo": nrm(ks[15], (DEPTH, MEM_WIDTH, D_MODEL), MEM_WIDTH),
        "g_ffn": gain(ks[16], (DEPTH, D_MODEL)),
        "w_router": nrm(ks[17], (DEPTH, D_MODEL, N_EXPERTS), D_MODEL),
        "w_gate": nrm(ks[18], (DEPTH, N_EXPERTS, D_MODEL, D_FF), D_MODEL),
        "w_up": nrm(ks[19], (DEPTH, N_EXPERTS, D_MODEL, D_FF), D_MODEL),
        "w_down": nrm(ks[20], (DEPTH, N_EXPERTS, D_FF, D_MODEL), D_FF),
        "g_final": gain(ks[21], (D_MODEL,)),
    }


def reference(x_prompt, x_sample, mem_prompt, mem_sample, g_mix, w_in, rpb, w_pool, pool_scale, w_out,
              g_mq, g_mkv, w_mq, w_mk, w_mv, w_mo, g_ffn, w_router, w_gate, w_up, w_down, g_final):
    y_prompt = _trunk(x_prompt, mem_prompt, g_mix, w_in, rpb, w_pool, pool_scale, w_out, g_mq, g_mkv,
                      w_mq, w_mk, w_mv, w_mo, g_ffn, w_router, w_gate, w_up, w_down, g_final)
    y_sample = _trunk(x_sample, mem_sample, g_mix, w_in, rpb, w_pool, pool_scale, w_out, g_mq, g_mkv,
                      w_mq, w_mk, w_mv, w_mo, g_ffn, w_router, w_gate, w_up, w_down, g_final)
    return (y_prompt, y_sample)
```

```python
import functools

import jax
import jax.numpy as jnp
from jax import lax
from jax.experimental import pallas as pl
from jax.experimental.pallas import tpu as pltpu

F32 = jnp.float32
BF16 = jnp.bfloat16

D_MODEL = 2048
GRID_W = 64
NA_WIDTH = 1024
NA_HEAD_DIM = 64
NA_HEADS = 16
WIN_ROWS = 8
WIN_COLS = 16
POOL_WINDOWS = (2, 4, 8, 16)
POOL_WIDTH = 1024
POOL_GROUP = 256
MEM_HEADS = 4
MEM_HEAD_DIM = 128
MEM_WIDTH = 512
N_EXPERTS = 16
EC_CAPACITY_FACTOR = 2
EPS = 1e-6
NEG_INF = -1e30

LANES = 128
POOL_HALO = 16
XH_WIDTH = 2 * D_MODEL + LANES
VMEM_LIMIT = 56 * 1024 * 1024


def _cparams(sem, vmem=VMEM_LIMIT):
    return pltpu.CompilerParams(dimension_semantics=sem, vmem_limit_bytes=vmem)


def _const_spec(shape):
    nd = len(shape)
    return pl.BlockSpec(shape, lambda *_: (0,) * nd, pipeline_mode=pl.Buffered(1))


def _rms(v, g):
    return v * lax.rsqrt(jnp.mean(v * v, axis=-1, keepdims=True) + EPS) * g


def _in_proj_kernel(x_ref, g_ref, w_ref, q_ref, k_ref, v_ref, u_ref):
    h = _rms(x_ref[...], g_ref[...]).astype(BF16)
    for j, o_ref in enumerate((q_ref, k_ref, v_ref, u_ref)):
        o_ref[...] = jnp.dot(h, w_ref[:, j * NA_WIDTH:(j + 1) * NA_WIDTH],
                             preferred_element_type=F32).astype(BF16)


def _in_proj(x2d, g_mix, w_in_bf):
    n = x2d.shape[0]
    tm = min(512, n)
    out = jax.ShapeDtypeStruct((n, NA_WIDTH), BF16)
    row = lambda i: (i, 0)
    return pl.pallas_call(
        _in_proj_kernel,
        grid=(n // tm,),
        in_specs=[pl.BlockSpec((tm, D_MODEL), row),
                  _const_spec((1, D_MODEL)),
                  _const_spec((D_MODEL, 4 * NA_WIDTH))],
        out_specs=[pl.BlockSpec((tm, NA_WIDTH), row)] * 4,
        out_shape=[out] * 4,
        compiler_params=_cparams(("parallel",)),
        name="in_proj",
    )(x2d, g_mix.reshape(1, D_MODEL), w_in_bf)


def _na_kernel(q_ref, k_ref, v_ref, b_ref, o_ref, *, rows_per_step, n_rows):
    rb = pl.program_id(2)
    lane = lax.broadcasted_iota(jnp.int32, (1, LANES), 1)
    m0 = jnp.where(lane < NA_HEAD_DIM, 1.0, 0.0).astype(BF16)
    m1 = jnp.where(lane >= NA_HEAD_DIM, 1.0, 0.0).astype(BF16)
    lane_f = lax.broadcasted_iota(jnp.int32, (GRID_W, LANES), 1)
    scale = NA_HEAD_DIM ** -0.5

    def body(i, carry):
        r = rb * rows_per_step + i
        r0 = jnp.clip(r - WIN_ROWS // 2, 0, n_rows - WIN_ROWS)
        d = r - r0
        qr = q_ref[0, pl.ds(pl.multiple_of(i * GRID_W, GRID_W), GRID_W), :]
        qq = jnp.concatenate([qr * m0, qr * m1], axis=0)
        kstart = pl.multiple_of(r0 * GRID_W, GRID_W)
        kw = k_ref[0, pl.ds(kstart, WIN_ROWS * GRID_W), :]
        vw = v_ref[0, pl.ds(kstart, WIN_ROWS * GRID_W), :]
        s = lax.dot_general(qq, kw, (((1,), (1,)), ((), ())), preferred_element_type=F32)
        s = s * scale + b_ref[0, d]
        m = jnp.max(s, axis=-1, keepdims=True)
        p = jnp.exp(s - m)
        l = jnp.sum(p, axis=-1, keepdims=True)
        o = jnp.dot(p.astype(BF16), vw, preferred_element_type=F32) / l
        out = jnp.where(lane_f < NA_HEAD_DIM, o[:GRID_W], o[GRID_W:])
        o_ref[0, pl.ds(pl.multiple_of(i * GRID_W, GRID_W), GRID_W), :] = out.astype(BF16)
        return carry

    lax.fori_loop(0, rows_per_step, body, 0)


def _na_bias_table(rpb):
    cols = jnp.arange(GRID_W)
    c0 = jnp.clip(cols - WIN_COLS // 2, 0, GRID_W - WIN_COLS)
    col_mask = (cols[None, :] >= c0[:, None]) & (cols[None, :] < c0[:, None] + WIN_COLS)
    dc_idx = jnp.clip(cols[None, :] - cols[:, None], -(WIN_COLS - 1), WIN_COLS - 1) + WIN_COLS - 1
    d = jnp.arange(WIN_ROWS)
    dr_idx = -d[:, None] + jnp.arange(WIN_ROWS)[None, :] + WIN_ROWS - 1
    t = rpb.astype(F32)[:, dr_idx]
    t = t[:, :, :, dc_idx]
    t = jnp.where(col_mask[None, None, None], t, NEG_INF)
    t = t.transpose(0, 1, 3, 2, 4).reshape(NA_HEADS, WIN_ROWS, GRID_W, WIN_ROWS * GRID_W)
    t = t.reshape(NA_HEADS // 2, 2, WIN_ROWS, GRID_W, WIN_ROWS * GRID_W).transpose(0, 2, 1, 3, 4)
    return t.reshape(NA_HEADS // 2, WIN_ROWS, 2 * GRID_W, WIN_ROWS * GRID_W)


def _na_attention(q, k, v, bias, b, t):
    n_rows = t // GRID_W
    assert n_rows >= WIN_ROWS
    rows_per_step = min(16, n_rows)
    tq = rows_per_step * GRID_W
    q3, k3, v3 = (a.reshape(b, t, NA_WIDTH) for a in (q, k, v))
    n_pairs = NA_HEADS // 2
    kern = functools.partial(_na_kernel, rows_per_step=rows_per_step, n_rows=n_rows)
    out = pl.pallas_call(
        kern,
        grid=(b, n_pairs, n_rows // rows_per_step),
        in_specs=[pl.BlockSpec((1, tq, LANES), lambda bi, p, r: (bi, r, p)),
                  pl.BlockSpec((1, t, LANES), lambda bi, p, r: (bi, 0, p)),
                  pl.BlockSpec((1, t, LANES), lambda bi, p, r: (bi, 0, p)),
                  pl.BlockSpec((1, WIN_ROWS, 2 * GRID_W, WIN_ROWS * GRID_W),
                               lambda bi, p, r: (p, 0, 0, 0))],
        out_specs=pl.BlockSpec((1, tq, LANES), lambda bi, p, r: (bi, r, p)),
        out_shape=jax.ShapeDtypeStruct((b, t, NA_WIDTH), BF16),
        compiler_params=_cparams(("parallel", "parallel", "parallel")),
        name="na_attention",
    )(q3, k3, v3, bias)
    return out.reshape(b * t, NA_WIDTH)


def _mem_kv_kernel(m_ref, g_ref, wk_ref, wv_ref, k_ref, v_ref):
    h = _rms(m_ref[0], g_ref[...]).astype(BF16)
    k_ref[0] = jnp.dot(h, wk_ref[...], preferred_element_type=F32).astype(BF16)
    v_ref[0] = jnp.dot(h, wv_ref[...], preferred_element_type=F32).astype(BF16)


def _mem_kv(mem, g_mkv, w_mk_bf, w_mv_bf):
    b, m, _ = mem.shape
    out = jax.ShapeDtypeStruct((b, m, MEM_WIDTH), BF16)
    return pl.pallas_call(
        _mem_kv_kernel,
        grid=(b,),
        in_specs=[pl.BlockSpec((1, m, D_MODEL), lambda i: (i, 0, 0)),
                  _const_spec((1, D_MODEL)),
                  _const_spec((D_MODEL, MEM_WIDTH)),
                  _const_spec((D_MODEL, MEM_WIDTH))],
        out_specs=[pl.BlockSpec((1, m, MEM_WIDTH), lambda i: (i, 0, 0))] * 2,
        out_shape=[out] * 2,
        compiler_params=_cparams(("parallel",)),
        name="mem_kv",
    )(mem, g_mkv.reshape(1, D_MODEL), w_mk_bf, w_mv_bf)


def _mix_kernel(x_ref, ona_ref, up_ref, uc_ref, un_ref, mk_ref, mv_ref,
                wpool_ref, pscale_ref, wout_ref, gmq_ref, wmq_ref, wmo_ref, gffn_ref,
                wrh_ref, wrl_ref, xh_ref, aff_ref, *, tm, seq):
    t0 = pl.program_id(1) * tm
    width = tm + 2 * POOL_HALO
    uc = uc_ref[0]
    ucat = jnp.concatenate([up_ref[0], uc, un_ref[0]], axis=0)
    pos = t0 + lax.broadcasted_iota(jnp.int32, (tm, width), 0)
    src = t0 - POOL_HALO + lax.broadcasted_iota(jnp.int32, (tm, width), 1)
    pcol = t0 + lax.broadcasted_iota(jnp.int32, (tm, 1), 0)
    pools = []
    for g, w in enumerate(POOL_WINDOWS):
        sl = slice(g * POOL_GROUP, (g + 1) * POOL_GROUP)
        lo = jnp.maximum(pos - w // 2, 0)
        hi = jnp.minimum(pos + w // 2, seq)
        band = jnp.where(src >= lo, jnp.where(src < hi, 1.0, 0.0), 0.0).astype(BF16)
        wsum = jnp.dot(band, ucat[:, sl], preferred_element_type=F32)
        cnt = (jnp.minimum(pcol + w // 2, seq) - jnp.maximum(pcol - w // 2, 0)).astype(F32)
        dlt = (wsum / cnt - uc[:, sl].astype(F32)).astype(BF16)
        pools.append(jnp.dot(dlt, wpool_ref[g], preferred_element_type=F32))
    o_pool = (jnp.concatenate(pools, axis=1) * pscale_ref[...]).astype(BF16)

    x1 = (x_ref[0]
          + jnp.dot(ona_ref[0], wout_ref[:NA_WIDTH, :], preferred_element_type=F32)
          + jnp.dot(o_pool, wout_ref[NA_WIDTH:, :], preferred_element_type=F32))

    hq = _rms(x1, gmq_ref[...]).astype(BF16)
    q = jnp.dot(hq, wmq_ref[...], preferred_element_type=F32)
    mk = mk_ref[0]
    mv = mv_ref[0]
    heads = []
    for h in range(MEM_HEADS):
        sl = slice(h * MEM_HEAD_DIM, (h + 1) * MEM_HEAD_DIM)
        s = lax.dot_general(q[:, sl].astype(BF16), mk[:, sl], (((1,), (1,)), ((), ())),
                            preferred_element_type=F32) * (MEM_HEAD_DIM ** -0.5)
        p = jnp.exp(s - jnp.max(s, axis=-1, keepdims=True))
        l = jnp.sum(p, axis=-1, keepdims=True)
        heads.append(jnp.dot(p.astype(BF16), mv[:, sl], preferred_element_type=F32) / l)
    o_mem = jnp.concatenate(heads, axis=1).astype(BF16)
    x2 = x1 + jnp.dot(o_mem, wmo_ref[...], preferred_element_type=F32)

    h3 = _rms(x2, gffn_ref[...])
    h_hi = h3.astype(BF16)
    h_lo = (h3 - h_hi.astype(F32)).astype(BF16)
    logits = (jnp.dot(h_hi, wrh_ref[...], preferred_element_type=F32)
              + jnp.dot(h_lo, wrh_ref[...], preferred_element_type=F32)
              + jnp.dot(h_hi, wrl_ref[...], preferred_element_type=F32))
    lane = lax.broadcasted_iota(jnp.int32, (tm, LANES), 1)
    logits = jnp.where(lane < N_EXPERTS, logits, NEG_INF)
    ex = jnp.exp(logits - jnp.max(logits, axis=-1, keepdims=True))
    aff = ex / jnp.sum(ex, axis=-1, keepdims=True)

    xh_ref[:, 0:D_MODEL] = x2
    xh_ref[:, D_MODEL:2 * D_MODEL] = h3
    xh_ref[:, 2 * D_MODEL:] = aff
    aff_ref[...] = aff[:, :N_EXPERTS]


def _mix(x, ona, u, mk, mv, wts, b, t):
    tm = min(256, t)
    nt = t // tm
    hb = tm // POOL_HALO
    last_hb = t // POOL_HALO - 1
    x3 = x
    ona3 = ona.reshape(b, t, NA_WIDTH)
    u3 = u.reshape(b, t, POOL_WIDTH)
    kern = functools.partial(_mix_kernel, tm=tm, seq=t)
    n = b * t
    return pl.pallas_call(
        kern,
        grid=(b, nt),
        in_specs=[
            pl.BlockSpec((1, tm, D_MODEL), lambda bi, i: (bi, i, 0)),
            pl.BlockSpec((1, tm, NA_WIDTH), lambda bi, i: (bi, i, 0)),
            pl.BlockSpec((1, POOL_HALO, POOL_WIDTH), lambda bi, i: (bi, jnp.maximum(i * hb - 1, 0), 0)),
            pl.BlockSpec((1, tm, POOL_WIDTH), lambda bi, i: (bi, i, 0)),
            pl.BlockSpec((1, POOL_HALO, POOL_WIDTH),
                         lambda bi, i: (bi, jnp.minimum((i + 1) * hb, last_hb), 0)),
            pl.BlockSpec((1, mk.shape[1], MEM_WIDTH), lambda bi, i: (bi, 0, 0)),
            pl.BlockSpec((1, mk.shape[1], MEM_WIDTH), lambda bi, i: (bi, 0, 0)),
            _const_spec((len(POOL_WINDOWS), POOL_GROUP, POOL_GROUP)),
            _const_spec((1, POOL_WIDTH)),
            _const_spec((D_MODEL, D_MODEL)),
            _const_spec((1, D_MODEL)),
            _const_spec((D_MODEL, MEM_WIDTH)),
            _const_spec((MEM_WIDTH, D_MODEL)),
            _const_spec((1, D_MODEL)),
            _const_spec((D_MODEL, LANES)),
            _const_spec((D_MODEL, LANES)),
        ],
        out_specs=[pl.BlockSpec((tm, XH_WIDTH), lambda bi, i: (bi * nt + i, 0)),
                   pl.BlockSpec((tm, N_EXPERTS), lambda bi, i: (bi * nt + i, 0))],
        out_shape=[jax.ShapeDtypeStruct((n, XH_WIDTH), F32),
                   jax.ShapeDtypeStruct((n, N_EXPERTS), F32)],
        compiler_params=_cparams(("parallel", "parallel")),
        name="mix_mem_router",
    )(x3, ona3, u3, u3, u3, mk, mv, *wts)


def _route_kernel(a_ref, idx_ref, *, nch, cap):
    a = a_ref[0]
    bits = pltpu.bitcast(a, jnp.int32)
    capf = jnp.float32(cap)

    def total(v):
        return jnp.sum(jnp.sum(v, axis=1, keepdims=True), axis=0, keepdims=True)

    def search(b, lo):
        cand = lo | jnp.left_shift(jnp.int32(1), 30 - b)
        cnt = total(jnp.where(bits >= cand, 1.0, 0.0))
        return jnp.where(cnt >= capf, cand, lo)

    thr = lax.fori_loop(0, 31, search, jnp.zeros((1, 1), jnp.int32))
    gt = bits > thr
    eq = bits == thr
    need = capf - total(jnp.where(gt, 1.0, 0.0))

    kr = lax.broadcasted_iota(jnp.int32, (LANES, LANES), 0)
    kc = lax.broadcasted_iota(jnp.int32, (LANES, LANES), 1)
    tri_lane = jnp.where(kr <= kc, 1.0, 0.0).astype(BF16)
    cr = lax.broadcasted_iota(jnp.int32, (nch, nch), 0)
    cc = lax.broadcasted_iota(jnp.int32, (nch, nch), 1)
    tri_excl = jnp.where(cc < cr, 1.0, 0.0).astype(BF16)
    tri_incl = jnp.where(cc <= cr, 1.0, 0.0).astype(BF16)

    def prefix(maskf, chunk_tri):
        within = jnp.dot(maskf.astype(BF16), tri_lane, preferred_element_type=F32)
        tot = jnp.broadcast_to(within[:, LANES - 1:LANES], (nch, LANES))
        before = jnp.dot(chunk_tri, tot.astype(BF16), preferred_element_type=F32)
        return within, before

    eqf = jnp.where(eq, 1.0, 0.0)
    eq_within, eq_before = prefix(eqf, tri_excl)
    eq_rank = eq_before + eq_within - eqf
    self_ = jnp.where(gt, 1.0, jnp.where(eq, jnp.where(eq_rank < need, 1.0, 0.0), 0.0))

    within, cs_incl = prefix(self_, tri_incl)
    cs_col = cs_incl[:, 0:1]
    j = lax.broadcasted_iota(jnp.int32, (1, cap), 1).astype(F32)
    done = cs_col <= j
    chunk = jnp.sum(jnp.where(done, 1.0, 0.0), axis=0, keepdims=True)
    cs_prev = jnp.max(jnp.where(done, cs_col, 0.0), axis=0, keepdims=True)
    crow = lax.broadcasted_iota(jnp.int32, (nch, cap), 0).astype(F32)
    onehot = jnp.where(crow == chunk, 1.0, 0.0).astype(BF16)
    wsel = jnp.dot(within.T.astype(BF16), onehot, preferred_element_type=F32)
    rank = j - cs_prev
    inner = jnp.sum(jnp.where(wsel <= rank, 1.0, 0.0), axis=0, keepdims=True)
    idx_ref[0] = (chunk * LANES + inner).astype(jnp.int32)


def _route(aff, n):
    cap = EC_CAPACITY_FACTOR * n // N_EXPERTS
    nch = n // LANES
    aff_t = aff.T.reshape(N_EXPERTS, nch, LANES)
    kern = functools.partial(_route_kernel, nch=nch, cap=cap)
    idx = pl.pallas_call(
        kern,
        grid=(N_EXPERTS,),
        in_specs=[pl.BlockSpec((1, nch, LANES), lambda e: (e, 0, 0))],
        out_specs=pl.BlockSpec((1, 1, cap), lambda e: (e, 0, 0)),
        out_shape=jax.ShapeDtypeStruct((N_EXPERTS, 1, cap), jnp.int32),
        compiler_params=_cparams(("parallel",)),
        name="route",
    )(aff_t)
    return idx.reshape(N_EXPERTS * cap), cap


def _moe_kernel(idx_ref, xh_in_ref, wg_ref, wu_ref, wd_ref, xh_ref, xbuf, gsem, ssem, *, tm, tiles, cap):
    del xh_in_ref
    e = pl.program_id(0)
    i = pl.program_id(1)
    slot = i % 2
    base = e * cap + i * tm

    def gather_copy(tok, r, sl):
        return pltpu.make_async_copy(xh_ref.at[pl.ds(tok, 1), :], xbuf.at[sl, pl.ds(r, 1), :], gsem.at[sl])

    def scatter_copy(tok, r, sl):
        return pltpu.make_async_copy(xbuf.at[sl, pl.ds(r, 1), pl.ds(0, D_MODEL)],
                                     xh_ref.at[pl.ds(tok, 1), pl.ds(0, D_MODEL)], ssem.at[sl])

    def for_rows(first, sl, fn):
        def body(r, c):
            fn(idx_ref[first + r], r, sl)
            return c
        lax.fori_loop(0, tm, body, 0)

    @pl.when((e > 0) | (i > 0))
    def _():
        prev_first = jnp.where(i > 0, base - tm, base - tm)
        for_rows(prev_first, 1 - slot if tiles > 1 else slot,
                 lambda tok, r, sl: scatter_copy(tok, r, sl).wait())

    @pl.when(i == 0)
    def _():
        for_rows(base, slot, lambda tok, r, sl: gather_copy(tok, r, sl).start())

    for_rows(base, slot, lambda tok, r, sl: gather_copy(tok, r, sl).wait())

    if tiles > 1:
        @pl.when(i + 1 < tiles)
        def _():
            for_rows(base + tm, 1 - slot, lambda tok, r, sl: gather_copy(tok, r, sl).start())

    xs = xbuf[slot, :, D_MODEL:2 * D_MODEL].astype(BF16)
    a = jnp.dot(xs, wg_ref[0], preferred_element_type=F32)
    up = jnp.dot(xs, wu_ref[0], preferred_element_type=F32)
    hid = (a * jax.nn.sigmoid(a) * up).astype(BF16)
    y = jnp.dot(hid, wd_ref[0], preferred_element_type=F32)
    lane = lax.broadcasted_iota(jnp.int32, (tm, LANES), 1)
    gate = jnp.sum(jnp.where(lane == e, xbuf[slot, :, 2 * D_MODEL:], 0.0), axis=1, keepdims=True)
    xbuf[slot, :, 0:D_MODEL] = xbuf[slot, :, 0:D_MODEL] + y * gate

    for_rows(base, slot, lambda tok, r, sl: scatter_copy(tok, r, sl).start())

    @pl.when((e == N_EXPERTS - 1) & (i == tiles - 1))
    def _():
        for_rows(base, slot, lambda tok, r, sl: scatter_copy(tok, r, sl).wait())


def _moe(idx, xh, wg_bf, wu_bf, wd_bf, cap, tm):
    tiles = cap // tm
    assert tiles == 1 or tiles % 2 == 0
    n = xh.shape[0]
    kern = functools.partial(_moe_kernel, tm=tm, tiles=tiles, cap=cap)
    wspec = pl.BlockSpec((1, D_MODEL, D_MODEL), lambda e, i, idx_ref: (e, 0, 0), pipeline_mode=pl.Buffered(1))
    return pl.pallas_call(
        kern,
        grid_spec=pltpu.PrefetchScalarGridSpec(
            num_scalar_prefetch=1,
            grid=(N_EXPERTS, tiles),
            in_specs=[pl.BlockSpec(memory_space=pl.ANY), wspec, wspec, wspec],
            out_specs=pl.BlockSpec(memory_space=pl.ANY),
            scratch_shapes=[pltpu.VMEM((2, tm, XH_WIDTH), F32),
                            pltpu.SemaphoreType.DMA((2,)),
                            pltpu.SemaphoreType.DMA((2,))],
        ),
        out_shape=jax.ShapeDtypeStruct((n, XH_WIDTH), F32),
        input_output_aliases={1: 0},
        compiler_params=_cparams(("arbitrary", "arbitrary")),
        name="moe_ffn",
    )(idx, xh, wg_bf, wu_bf, wd_bf)


def _final_kernel(x_ref, g_ref, o_ref):
    o_ref[...] = _rms(x_ref[...], g_ref[...])


def _final_norm(xh, g_final):
    n = xh.shape[0]
    tm = min(512, n)
    return pl.pallas_call(
        _final_kernel,
        grid=(n // tm,),
        in_specs=[pl.BlockSpec((tm, D_MODEL), lambda i: (i, 0)), _const_spec((1, D_MODEL))],
        out_specs=pl.BlockSpec((tm, D_MODEL), lambda i: (i, 0)),
        out_shape=jax.ShapeDtypeStruct((n, D_MODEL), F32),
        compiler_params=_cparams(("parallel",)),
        name="final_norm",
    )(xh, g_final.reshape(1, D_MODEL))


def _trunk(x, mem, p, moe_tm=256):
    b, t, _ = x.shape
    n = b * t
    q, k, v, u = _in_proj(x.reshape(n, D_MODEL), p["g_mix"], p["w_in"])
    ona = _na_attention(q, k, v, p["na_bias"], b, t)
    mk, mv = _mem_kv(mem, p["g_mkv"], p["w_mk"], p["w_mv"])
    wts = (p["w_pool"], p["pool_scale"], p["w_out"], p["g_mq"], p["w_mq"], p["w_mo"], p["g_ffn"],
           p["w_router_hi"], p["w_router_lo"])
    xh, aff = _mix(x, ona, u, mk, mv, wts, b, t)
    idx, cap = _route(aff, n)
    xh = _moe(idx, xh, p["w_gate"], p["w_up"], p["w_down"], cap, min(moe_tm, cap))
    return _final_norm(xh, p["g_final"]).reshape(b, t, D_MODEL)


def _prepare(g_mix, w_in, rpb, w_pool, pool_scale, w_out, g_mq, g_mkv, w_mq, w_mk, w_mv, w_mo,
             g_ffn, w_router, w_gate, w_up, w_down, g_final):
    wr = jnp.pad(w_router[0], ((0, 0), (0, LANES - N_EXPERTS)))
    wr_hi = wr.astype(BF16)
    wr_lo = (wr - wr_hi.astype(F32)).astype(BF16)
    row = lambda a: a.reshape(1, -1)
    return dict(
        g_mix=g_mix[0], w_in=w_in[0].astype(BF16), na_bias=_na_bias_table(rpb[0]),
        w_pool=w_pool[0].astype(BF16), pool_scale=row(pool_scale[0]), w_out=w_out[0].astype(BF16),
        g_mq=row(g_mq[0]), g_mkv=g_mkv[0], w_mq=w_mq[0].astype(BF16), w_mk=w_mk[0].astype(BF16),
        w_mv=w_mv[0].astype(BF16), w_mo=w_mo[0].astype(BF16), g_ffn=row(g_ffn[0]),
        w_router_hi=wr_hi, w_router_lo=wr_lo,
        w_gate=w_gate[0].astype(BF16), w_up=w_up[0].astype(BF16), w_down=w_down[0].astype(BF16),
        g_final=g_final)


def kernel(x_prompt, x_sample, mem_prompt, mem_sample, g_mix, w_in, rpb, w_pool, pool_scale, w_out,
           g_mq, g_mkv, w_mq, w_mk, w_mv, w_mo, g_ffn, w_router, w_gate, w_up, w_down, g_final):
    p = _prepare(g_mix, w_in, rpb, w_pool, pool_scale, w_out, g_mq, g_mkv, w_mq, w_mk, w_mv, w_mo,
                 g_ffn, w_router, w_gate, w_up, w_down, g_final)
    return _trunk(x_prompt, mem_prompt, p), _trunk(x_sample, mem_sample, p)
```

```python
import functools

import jax
import jax.numpy as jnp
from jax import lax
from jax.experimental import pallas as pl
from jax.experimental.pallas import tpu as pltpu

F32 = jnp.float32
BF16 = jnp.bfloat16

D_MODEL = 2048
GRID_W = 64
NA_WIDTH = 1024
NA_HEAD_DIM = 64
NA_HEADS = 16
WIN_ROWS = 8
WIN_COLS = 16
POOL_WINDOWS = (2, 4, 8, 16)
POOL_WIDTH = 1024
POOL_GROUP = 256
MEM_HEADS = 4
MEM_HEAD_DIM = 128
MEM_WIDTH = 512
N_EXPERTS = 16
EC_CAPACITY_FACTOR = 2
EPS = 1e-6
NEG_INF = -1e30
LOG2_E = 1.4426950408889634

LANES = 128
POOL_HALO = 16
XH_WIDTH = 2 * D_MODEL + LANES
VMEM_LIMIT = 56 * 1024 * 1024
MOE_CHUNKS = 8
NA_GROUP = 16


def _cparams(sem, vmem=VMEM_LIMIT):
    return pltpu.CompilerParams(dimension_semantics=sem, vmem_limit_bytes=vmem)


def _const_spec(shape):
    nd = len(shape)
    return pl.BlockSpec(shape, lambda *_: (0,) * nd, pipeline_mode=pl.Buffered(1))


def _rms(v, g):
    return v * lax.rsqrt(jnp.mean(v * v, axis=-1, keepdims=True) + EPS) * g


def _in_proj_kernel(x_ref, g_ref, w_ref, q_ref, k_ref, v_ref, u_ref):
    h = _rms(x_ref[...], g_ref[...]).astype(BF16)
    for j, o_ref in enumerate((q_ref, k_ref, v_ref, u_ref)):
        o_ref[...] = jnp.dot(h, w_ref[:, j * NA_WIDTH:(j + 1) * NA_WIDTH],
                             preferred_element_type=F32).astype(BF16)


def _in_proj(x2d, g_mix, w_in_bf):
    n = x2d.shape[0]
    tm = min(512, n)
    out = jax.ShapeDtypeStruct((n, NA_WIDTH), BF16)
    row = lambda i: (i, 0)
    return pl.pallas_call(
        _in_proj_kernel,
        grid=(n // tm,),
        in_specs=[pl.BlockSpec((tm, D_MODEL), row),
                  _const_spec((1, D_MODEL)),
                  _const_spec((D_MODEL, 4 * NA_WIDTH))],
        out_specs=[pl.BlockSpec((tm, NA_WIDTH), row)] * 4,
        out_shape=[out] * 4,
        compiler_params=_cparams(("parallel",)),
        name="in_proj",
    )(x2d, g_mix.reshape(1, D_MODEL), w_in_bf)


def _na_kernel(q_ref, k_ref, v_ref, b_ref, o_ref, *, rows_per_step, n_rows):
    rb = pl.program_id(2)
    lane = lax.broadcasted_iota(jnp.int32, (1, LANES), 1)
    m0 = jnp.where(lane < NA_HEAD_DIM, 1.0, 0.0).astype(BF16)
    m1 = jnp.where(lane >= NA_HEAD_DIM, 1.0, 0.0).astype(BF16)
    lane_f = lax.broadcasted_iota(jnp.int32, (GRID_W, LANES), 1)
    scale = NA_HEAD_DIM ** -0.5 * LOG2_E

    def group(gi, carry):
        rows, vws, scores = [], [], []
        for j in range(NA_GROUP):
            i = gi * NA_GROUP + j
            r = rb * rows_per_step + i
            r0 = jnp.clip(r - WIN_ROWS // 2, 0, n_rows - WIN_ROWS)
            off = pl.multiple_of(i * GRID_W, GRID_W)
            qr = q_ref[0, pl.ds(off, GRID_W), :]
            qq = jnp.concatenate([qr * m0, qr * m1], axis=0)
            kstart = pl.multiple_of(r0 * GRID_W, GRID_W)
            kw = k_ref[0, pl.ds(kstart, WIN_ROWS * GRID_W), :]
            vws.append(v_ref[0, pl.ds(kstart, WIN_ROWS * GRID_W), :])
            scores.append(lax.dot_general(qq, kw, (((1,), (1,)), ((), ())), preferred_element_type=F32))
            rows.append((off, r - r0))
        probs, sums = [], []
        for j in range(NA_GROUP):
            s = scores[j] * scale + b_ref[0, rows[j][1]]
            p = jnp.exp2(s - jnp.max(s, axis=-1, keepdims=True))
            sums.append(jnp.sum(p, axis=-1, keepdims=True))
            probs.append(p.astype(BF16))
        for j in range(NA_GROUP):
            o = jnp.dot(probs[j], vws[j], preferred_element_type=F32) / sums[j]
            out = jnp.where(lane_f < NA_HEAD_DIM, o[:GRID_W], o[GRID_W:])
            o_ref[0, pl.ds(rows[j][0], GRID_W), :] = out.astype(BF16)
        return carry

    lax.fori_loop(0, rows_per_step // NA_GROUP, group, 0)


def _na_bias_table(rpb):
    cols = jnp.arange(GRID_W)
    c0 = jnp.clip(cols - WIN_COLS // 2, 0, GRID_W - WIN_COLS)
    col_mask = (cols[None, :] >= c0[:, None]) & (cols[None, :] < c0[:, None] + WIN_COLS)
    dc_idx = jnp.clip(cols[None, :] - cols[:, None], -(WIN_COLS - 1), WIN_COLS - 1) + WIN_COLS - 1
    d = jnp.arange(WIN_ROWS)
    dr_idx = -d[:, None] + jnp.arange(WIN_ROWS)[None, :] + WIN_ROWS - 1
    t = rpb.astype(F32)[:, dr_idx]
    t = t[:, :, :, dc_idx]
    t = jnp.where(col_mask[None, None, None], t * LOG2_E, NEG_INF)
    t = t.transpose(0, 1, 3, 2, 4).reshape(NA_HEADS, WIN_ROWS, GRID_W, WIN_ROWS * GRID_W)
    t = t.reshape(NA_HEADS // 2, 2, WIN_ROWS, GRID_W, WIN_ROWS * GRID_W).transpose(0, 2, 1, 3, 4)
    return t.reshape(NA_HEADS // 2, WIN_ROWS, 2 * GRID_W, WIN_ROWS * GRID_W)


def _na_attention(q, k, v, bias, b, t):
    n_rows = t // GRID_W
    assert n_rows >= WIN_ROWS
    rows_per_step = min(16, n_rows)
    tq = rows_per_step * GRID_W
    q3, k3, v3 = (a.reshape(b, t, NA_WIDTH) for a in (q, k, v))
    n_pairs = NA_HEADS // 2
    kern = functools.partial(_na_kernel, rows_per_step=rows_per_step, n_rows=n_rows)
    out = pl.pallas_call(
        kern,
        grid=(b, n_pairs, n_rows // rows_per_step),
        in_specs=[pl.BlockSpec((1, tq, LANES), lambda bi, p, r: (bi, r, p)),
                  pl.BlockSpec((1, t, LANES), lambda bi, p, r: (bi, 0, p)),
                  pl.BlockSpec((1, t, LANES), lambda bi, p, r: (bi, 0, p)),
                  pl.BlockSpec((1, WIN_ROWS, 2 * GRID_W, WIN_ROWS * GRID_W),
                               lambda bi, p, r: (p, 0, 0, 0))],
        out_specs=pl.BlockSpec((1, tq, LANES), lambda bi, p, r: (bi, r, p)),
        out_shape=jax.ShapeDtypeStruct((b, t, NA_WIDTH), BF16),
        compiler_params=_cparams(("parallel", "parallel", "parallel")),
        name="na_attention",
    )(q3, k3, v3, bias)
    return out.reshape(b * t, NA_WIDTH)


def _mem_kv_kernel(m_ref, g_ref, wk_ref, wv_ref, k_ref, v_ref):
    h = _rms(m_ref[0], g_ref[...]).astype(BF16)
    k_ref[0] = jnp.dot(h, wk_ref[...], preferred_element_type=F32).astype(BF16)
    v_ref[0] = jnp.dot(h, wv_ref[...], preferred_element_type=F32).astype(BF16)


def _mem_kv(mem, g_mkv, w_mk_bf, w_mv_bf):
    b, m, _ = mem.shape
    out = jax.ShapeDtypeStruct((b, m, MEM_WIDTH), BF16)
    return pl.pallas_call(
        _mem_kv_kernel,
        grid=(b,),
        in_specs=[pl.BlockSpec((1, m, D_MODEL), lambda i: (i, 0, 0)),
                  _const_spec((1, D_MODEL)),
                  _const_spec((D_MODEL, MEM_WIDTH)),
                  _const_spec((D_MODEL, MEM_WIDTH))],
        out_specs=[pl.BlockSpec((1, m, MEM_WIDTH), lambda i: (i, 0, 0))] * 2,
        out_shape=[out] * 2,
        compiler_params=_cparams(("parallel",)),
        name="mem_kv",
    )(mem, g_mkv.reshape(1, D_MODEL), w_mk_bf, w_mv_bf)


def _mix_kernel(x_ref, ona_ref, up_ref, uc_ref, un_ref, mk_ref, mv_ref,
                wpool_ref, pscale_ref, wout_ref, gmq_ref, wmq_ref, wmo_ref, gffn_ref,
                wrh_ref, wrl_ref, xh_ref, aff_ref, *, tm, seq):
    t0 = pl.program_id(1) * tm
    width = tm + 2 * POOL_HALO
    uc = uc_ref[0]
    ucat = jnp.concatenate([up_ref[0], uc, un_ref[0]], axis=0)
    pos = t0 + lax.broadcasted_iota(jnp.int32, (tm, width), 0)
    src = t0 - POOL_HALO + lax.broadcasted_iota(jnp.int32, (tm, width), 1)
    pcol = t0 + lax.broadcasted_iota(jnp.int32, (tm, 1), 0)
    pools = []
    for g, w in enumerate(POOL_WINDOWS):
        sl = slice(g * POOL_GROUP, (g + 1) * POOL_GROUP)
        lo = jnp.maximum(pos - w // 2, 0)
        hi = jnp.minimum(pos + w // 2, seq)
        band = jnp.where(src >= lo, jnp.where(src < hi, 1.0, 0.0), 0.0).astype(BF16)
        wsum = jnp.dot(band, ucat[:, sl], preferred_element_type=F32)
        cnt = (jnp.minimum(pcol + w // 2, seq) - jnp.maximum(pcol - w // 2, 0)).astype(F32)
        dlt = (wsum / cnt - uc[:, sl].astype(F32)).astype(BF16)
        pools.append(jnp.dot(dlt, wpool_ref[g], preferred_element_type=F32))
    o_pool = (jnp.concatenate(pools, axis=1) * pscale_ref[...]).astype(BF16)

    x1 = (x_ref[0]
          + jnp.dot(ona_ref[0], wout_ref[:NA_WIDTH, :], preferred_element_type=F32)
          + jnp.dot(o_pool, wout_ref[NA_WIDTH:, :], preferred_element_type=F32))

    hq = _rms(x1, gmq_ref[...]).astype(BF16)
    q = jnp.dot(hq, wmq_ref[...], preferred_element_type=F32)
    mk = mk_ref[0]
    mv = mv_ref[0]
    heads = []
    for h in range(MEM_HEADS):
        sl = slice(h * MEM_HEAD_DIM, (h + 1) * MEM_HEAD_DIM)
        s = lax.dot_general(q[:, sl].astype(BF16), mk[:, sl], (((1,), (1,)), ((), ())),
                            preferred_element_type=F32) * (MEM_HEAD_DIM ** -0.5)
        p = jnp.exp(s - jnp.max(s, axis=-1, keepdims=True))
        l = jnp.sum(p, axis=-1, keepdims=True)
        heads.append(jnp.dot(p.astype(BF16), mv[:, sl], preferred_element_type=F32) / l)
    o_mem = jnp.concatenate(heads, axis=1).astype(BF16)
    x2 = x1 + jnp.dot(o_mem, wmo_ref[...], preferred_element_type=F32)

    h3 = _rms(x2, gffn_ref[...])
    h_hi = h3.astype(BF16)
    h_lo = (h3 - h_hi.astype(F32)).astype(BF16)
    logits = (jnp.dot(h_hi, wrh_ref[...], preferred_element_type=F32)
              + jnp.dot(h_lo, wrh_ref[...], preferred_element_type=F32)
              + jnp.dot(h_hi, wrl_ref[...], preferred_element_type=F32))
    lane = lax.broadcasted_iota(jnp.int32, (tm, LANES), 1)
    logits = jnp.where(lane < N_EXPERTS, logits, NEG_INF)
    ex = jnp.exp(logits - jnp.max(logits, axis=-1, keepdims=True))
    aff = ex / jnp.sum(ex, axis=-1, keepdims=True)

    xh_ref[:, 0:D_MODEL] = x2
    xh_ref[:, D_MODEL:2 * D_MODEL] = h3
    xh_ref[:, 2 * D_MODEL:] = aff
    aff_ref[...] = aff[:, :N_EXPERTS]


def _mix(x, ona, u, mk, mv, wts, b, t):
    tm = min(256, t)
    nt = t // tm
    hb = tm // POOL_HALO
    last_hb = t // POOL_HALO - 1
    x3 = x
    ona3 = ona.reshape(b, t, NA_WIDTH)
    u3 = u.reshape(b, t, POOL_WIDTH)
    kern = functools.partial(_mix_kernel, tm=tm, seq=t)
    n = b * t
    return pl.pallas_call(
        kern,
        grid=(b, nt),
        in_specs=[
            pl.BlockSpec((1, tm, D_MODEL), lambda bi, i: (bi, i, 0)),
            pl.BlockSpec((1, tm, NA_WIDTH), lambda bi, i: (bi, i, 0)),
            pl.BlockSpec((1, POOL_HALO, POOL_WIDTH), lambda bi, i: (bi, jnp.maximum(i * hb - 1, 0), 0)),
            pl.BlockSpec((1, tm, POOL_WIDTH), lambda bi, i: (bi, i, 0)),
            pl.BlockSpec((1, POOL_HALO, POOL_WIDTH),
                         lambda bi, i: (bi, jnp.minimum((i + 1) * hb, last_hb), 0)),
            pl.BlockSpec((1, mk.shape[1], MEM_WIDTH), lambda bi, i: (bi, 0, 0)),
            pl.BlockSpec((1, mk.shape[1], MEM_WIDTH), lambda bi, i: (bi, 0, 0)),
            _const_spec((len(POOL_WINDOWS), POOL_GROUP, POOL_GROUP)),
            _const_spec((1, POOL_WIDTH)),
            _const_spec((D_MODEL, D_MODEL)),
            _const_spec((1, D_MODEL)),
            _const_spec((D_MODEL, MEM_WIDTH)),
            _const_spec((MEM_WIDTH, D_MODEL)),
            _const_spec((1, D_MODEL)),
            _const_spec((D_MODEL, LANES)),
            _const_spec((D_MODEL, LANES)),
        ],
        out_specs=[pl.BlockSpec((tm, XH_WIDTH), lambda bi, i: (bi * nt + i, 0)),
                   pl.BlockSpec((tm, N_EXPERTS), lambda bi, i: (bi * nt + i, 0))],
        out_shape=[jax.ShapeDtypeStruct((n, XH_WIDTH), F32),
                   jax.ShapeDtypeStruct((n, N_EXPERTS), F32)],
        compiler_params=_cparams(("parallel", "parallel")),
        name="mix_mem_router",
    )(x3, ona3, u3, u3, u3, mk, mv, *wts)


def _route_kernel(a_ref, idx_ref, *, nch, cap):
    a = a_ref[0]
    bits = pltpu.bitcast(a, jnp.int32)
    capf = jnp.float32(cap)

    def total(v):
        return jnp.sum(jnp.sum(v, axis=1, keepdims=True), axis=0, keepdims=True)

    def search(b, lo):
        cand = lo | jnp.left_shift(jnp.int32(1), 30 - b)
        cnt = total(jnp.where(bits >= cand, 1.0, 0.0))
        return jnp.where(cnt >= capf, cand, lo)

    thr = lax.fori_loop(0, 31, search, jnp.zeros((1, 1), jnp.int32))
    gt = bits > thr
    eq = bits == thr
    need = capf - total(jnp.where(gt, 1.0, 0.0))

    kr = lax.broadcasted_iota(jnp.int32, (LANES, LANES), 0)
    kc = lax.broadcasted_iota(jnp.int32, (LANES, LANES), 1)
    tri_lane = jnp.where(kr <= kc, 1.0, 0.0).astype(BF16)
    cr = lax.broadcasted_iota(jnp.int32, (nch, nch), 0)
    cc = lax.broadcasted_iota(jnp.int32, (nch, nch), 1)
    tri_excl = jnp.where(cc < cr, 1.0, 0.0).astype(BF16)
    tri_incl = jnp.where(cc <= cr, 1.0, 0.0).astype(BF16)

    def prefix(maskf, chunk_tri):
        within = jnp.dot(maskf.astype(BF16), tri_lane, preferred_element_type=F32)
        tot = jnp.broadcast_to(within[:, LANES - 1:LANES], (nch, LANES))
        before = jnp.dot(chunk_tri, tot.astype(BF16), preferred_element_type=F32)
        return within, before

    eqf = jnp.where(eq, 1.0, 0.0)
    eq_within, eq_before = prefix(eqf, tri_excl)
    eq_rank = eq_before + eq_within - eqf
    self_ = jnp.where(gt, 1.0, jnp.where(eq, jnp.where(eq_rank < need, 1.0, 0.0), 0.0))

    within, cs_incl = prefix(self_, tri_incl)
    cs_col = cs_incl[:, 0:1]
    j = lax.broadcasted_iota(jnp.int32, (1, cap), 1).astype(F32)
    done = cs_col <= j
    chunk = jnp.sum(jnp.where(done, 1.0, 0.0), axis=0, keepdims=True)
    cs_prev = jnp.max(jnp.where(done, cs_col, 0.0), axis=0, keepdims=True)
    crow = lax.broadcasted_iota(jnp.int32, (nch, cap), 0).astype(F32)
    onehot = jnp.where(crow == chunk, 1.0, 0.0).astype(BF16)
    wsel = jnp.dot(within.T.astype(BF16), onehot, preferred_element_type=F32)
    rank = j - cs_prev
    inner = jnp.sum(jnp.where(wsel <= rank, 1.0, 0.0), axis=0, keepdims=True)
    idx_ref[0] = (chunk * LANES + inner).astype(jnp.int32)


def _route(aff, n):
    cap = EC_CAPACITY_FACTOR * n // N_EXPERTS
    nch = n // LANES
    aff_t = aff.T.reshape(N_EXPERTS, nch, LANES)
    kern = functools.partial(_route_kernel, nch=nch, cap=cap)
    idx = pl.pallas_call(
        kern,
        grid=(N_EXPERTS,),
        in_specs=[pl.BlockSpec((1, nch, LANES), lambda e: (e, 0, 0))],
        out_specs=pl.BlockSpec((1, 1, cap), lambda e: (e, 0, 0)),
        out_shape=jax.ShapeDtypeStruct((N_EXPERTS, 1, cap), jnp.int32),
        compiler_params=_cparams(("parallel",)),
        name="route",
    )(aff_t)
    return idx.reshape(N_EXPERTS * cap), cap


def _moe_kernel(idx_ref, xh_in_ref, wg_ref, wu_ref, wd_ref, xh_ref, xbuf, obuf, hid_ref, gsem, ssem,
                *, tm, tiles):
    del xh_in_ref
    e = pl.program_id(0)
    i = pl.program_id(1)
    s = e * tiles + i
    last = N_EXPERTS * tiles - 1
    slot = lax.rem(s, 2)
    other = 1 - slot
    base = s * tm

    def gather_copy(tok, r, sl):
        return pltpu.make_async_copy(xh_ref.at[pl.ds(tok, 1), :], xbuf.at[sl, pl.ds(r, 1), :], gsem.at[sl])

    def scatter_copy(tok, r, sl):
        return pltpu.make_async_copy(obuf.at[sl, pl.ds(r, 1), :],
                                     xh_ref.at[pl.ds(tok, 1), pl.ds(0, D_MODEL)], ssem.at[sl])

    def start_rows(copy, first, sl):
        def body(r, c):
            copy(idx_ref[first + r], r, sl).start()
            return c
        lax.fori_loop(0, tm, body, 0, unroll=8)

    def wait_rows(copy, sl):
        for _ in range(tm):
            copy(0, 0, sl).wait()

    @pl.when(s >= 1)
    def _():
        wait_rows(scatter_copy, slot)

    @pl.when(s == 0)
    def _():
        start_rows(gather_copy, base, slot)

    @pl.when((i == 0) & (s >= 1))
    def _():
        wait_rows(gather_copy, slot)
        start_rows(scatter_copy, base - tm, other)
        wait_rows(scatter_copy, other)
        start_rows(gather_copy, base, slot)

    wait_rows(gather_copy, slot)

    @pl.when(i == 0)
    def _():
        obuf[other] = xbuf[slot, :, 0:D_MODEL]

    wb_first = jnp.where(i == 0, base, base - tm)
    pf_first = jnp.where(i + 1 < tiles, base + tm, base)
    rows_per_chunk = tm // MOE_CHUNKS
    fw = D_MODEL // MOE_CHUNKS
    xs = xbuf[slot, :, D_MODEL:2 * D_MODEL].astype(BF16)
    for c in range(MOE_CHUNKS):
        a = jnp.dot(xs, wg_ref[0, :, c * fw:(c + 1) * fw], preferred_element_type=F32)
        up = jnp.dot(xs, wu_ref[0, :, c * fw:(c + 1) * fw], preferred_element_type=F32)
        hid_ref[:, c * fw:(c + 1) * fw] = (a * jax.nn.sigmoid(a) * up).astype(BF16)
        for r in range(c * rows_per_chunk, (c + 1) * rows_per_chunk):
            scatter_copy(idx_ref[wb_first + r], r, other).start()
            gather_copy(idx_ref[pf_first + r], r, other).start()
    y = jnp.dot(hid_ref[...], wd_ref[0], preferred_element_type=F32)
    lane = lax.broadcasted_iota(jnp.int32, (tm, LANES), 1)
    gate = jnp.sum(jnp.where(lane == e, xbuf[slot, :, 2 * D_MODEL:], 0.0), axis=1, keepdims=True)
    obuf[slot] = xbuf[slot, :, 0:D_MODEL] + y * gate

    @pl.when(s == last)
    def _():
        wait_rows(gather_copy, other)
        wait_rows(scatter_copy, other)
        start_rows(scatter_copy, base, slot)
        wait_rows(scatter_copy, slot)


def _moe(idx, xh, wg_bf, wu_bf, wd_bf, cap, tm):
    tiles = cap // tm
    assert tiles >= 2 and tm % MOE_CHUNKS == 0
    n = xh.shape[0]
    kern = functools.partial(_moe_kernel, tm=tm, tiles=tiles)
    wspec = pl.BlockSpec((1, D_MODEL, D_MODEL), lambda e, i, idx_ref: (e, 0, 0), pipeline_mode=pl.Buffered(1))
    return pl.pallas_call(
        kern,
        grid_spec=pltpu.PrefetchScalarGridSpec(
            num_scalar_prefetch=1,
            grid=(N_EXPERTS, tiles),
            in_specs=[pl.BlockSpec(memory_space=pl.ANY), wspec, wspec, wspec],
            out_specs=pl.BlockSpec(memory_space=pl.ANY),
            scratch_shapes=[pltpu.VMEM((2, tm, XH_WIDTH), F32),
                            pltpu.VMEM((2, tm, D_MODEL), F32),
                            pltpu.VMEM((tm, D_MODEL), BF16),
                            pltpu.SemaphoreType.DMA((2,)),
                            pltpu.SemaphoreType.DMA((2,))],
        ),
        out_shape=jax.ShapeDtypeStruct((n, XH_WIDTH), F32),
        input_output_aliases={1: 0},
        compiler_params=pltpu.CompilerParams(dimension_semantics=("arbitrary", "arbitrary"),
                                             vmem_limit_bytes=VMEM_LIMIT, disable_bounds_checks=True),
        name="moe_ffn",
    )(idx, xh, wg_bf, wu_bf, wd_bf)


def _final_kernel(x_ref, g_ref, o_ref):
    o_ref[...] = _rms(x_ref[...], g_ref[...])


def _final_norm(xh, g_final):
    n = xh.shape[0]
    tm = min(512, n)
    return pl.pallas_call(
        _final_kernel,
        grid=(n // tm,),
        in_specs=[pl.BlockSpec((tm, D_MODEL), lambda i: (i, 0)), _const_spec((1, D_MODEL))],
        out_specs=pl.BlockSpec((tm, D_MODEL), lambda i: (i, 0)),
        out_shape=jax.ShapeDtypeStruct((n, D_MODEL), F32),
        compiler_params=_cparams(("parallel",)),
        name="final_norm",
    )(xh, g_final.reshape(1, D_MODEL))


def _trunk(x, mem, p, moe_tm=256):
    b, t, _ = x.shape
    n = b * t
    q, k, v, u = _in_proj(x.reshape(n, D_MODEL), p["g_mix"], p["w_in"])
    ona = _na_attention(q, k, v, p["na_bias"], b, t)
    mk, mv = _mem_kv(mem, p["g_mkv"], p["w_mk"], p["w_mv"])
    wts = (p["w_pool"], p["pool_scale"], p["w_out"], p["g_mq"], p["w_mq"], p["w_mo"], p["g_ffn"],
           p["w_router_hi"], p["w_router_lo"])
    xh, aff = _mix(x, ona, u, mk, mv, wts, b, t)
    idx, cap = _route(aff, n)
    xh = _moe(idx, xh, p["w_gate"], p["w_up"], p["w_down"], cap, min(moe_tm, cap))
    return _final_norm(xh, p["g_final"]).reshape(b, t, D_MODEL)


def _prepare(g_mix, w_in, rpb, w_pool, pool_scale, w_out, g_mq, g_mkv, w_mq, w_mk, w_mv, w_mo,
             g_ffn, w_router, w_gate, w_up, w_down, g_final):
    wr = jnp.pad(w_router[0], ((0, 0), (0, LANES - N_EXPERTS)))
    wr_hi = wr.astype(BF16)
    wr_lo = (wr - wr_hi.astype(F32)).astype(BF16)
    row = lambda a: a.reshape(1, -1)
    return dict(
        g_mix=g_mix[0], w_in=w_in[0].astype(BF16), na_bias=_na_bias_table(rpb[0]),
        w_pool=w_pool[0].astype(BF16), pool_scale=row(pool_scale[0]), w_out=w_out[0].astype(BF16),
        g_mq=row(g_mq[0]), g_mkv=g_mkv[0], w_mq=w_mq[0].astype(BF16), w_mk=w_mk[0].astype(BF16),
        w_mv=w_mv[0].astype(BF16), w_mo=w_mo[0].astype(BF16), g_ffn=row(g_ffn[0]),
        w_router_hi=wr_hi, w_router_lo=wr_lo,
        w_gate=w_gate[0].astype(BF16), w_up=w_up[0].astype(BF16), w_down=w_down[0].astype(BF16),
        g_final=g_final)


def kernel(x_prompt, x_sample, mem_prompt, mem_sample, g_mix, w_in, rpb, w_pool, pool_scale, w_out,
           g_mq, g_mkv, w_mq, w_mk, w_mv, w_mo, g_ffn, w_router, w_gate, w_up, w_down, g_final):
    p = _prepare(g_mix, w_in, rpb, w_pool, pool_scale, w_out, g_mq, g_mkv, w_mq, w_mk, w_mv, w_mo,
                 g_ffn, w_router, w_gate, w_up, w_down, g_final)
    return _trunk(x_prompt, mem_prompt, p), _trunk(x_sample, mem_sample, p)
```

```python
import functools

import jax
import jax.numpy as jnp
from jax import lax
from jax.experimental import pallas as pl
from jax.experimental.pallas import tpu as pltpu

F32 = jnp.float32
BF16 = jnp.bfloat16

D_MODEL = 2048
GRID_W = 64
NA_WIDTH = 1024
NA_HEAD_DIM = 64
NA_HEADS = 16
WIN_ROWS = 8
WIN_COLS = 16
POOL_WINDOWS = (2, 4, 8, 16)
POOL_WIDTH = 1024
POOL_GROUP = 256
MEM_HEADS = 4
MEM_HEAD_DIM = 128
MEM_WIDTH = 512
N_EXPERTS = 16
EC_CAPACITY_FACTOR = 2
EPS = 1e-6
NEG_INF = -1e30
LOG2_E = 1.4426950408889634

LANES = 128
POOL_HALO = 16
XH_WIDTH = 2 * D_MODEL + LANES
VMEM_LIMIT = 56 * 1024 * 1024
MOE_CHUNKS = 8
NA_ROWS = 32
NA_GROUP = 2
MIX_PARTS = 2


def _cparams(sem, vmem=VMEM_LIMIT):
    return pltpu.CompilerParams(dimension_semantics=sem, vmem_limit_bytes=vmem)


def _const_spec(shape):
    nd = len(shape)
    return pl.BlockSpec(shape, lambda *_: (0,) * nd, pipeline_mode=pl.Buffered(1))


def _rms(v, g):
    return v * lax.rsqrt(jnp.mean(v * v, axis=-1, keepdims=True) + EPS) * g


def _in_proj_kernel(x_ref, g_ref, w_ref, q_ref, k_ref, v_ref, u_ref):
    h = _rms(x_ref[...], g_ref[...]).astype(BF16)
    for j, o_ref in enumerate((q_ref, k_ref, v_ref, u_ref)):
        o_ref[...] = jnp.dot(h, w_ref[:, j * NA_WIDTH:(j + 1) * NA_WIDTH],
                             preferred_element_type=F32).astype(BF16)


def _in_proj(x2d, g_mix, w_in_bf):
    n = x2d.shape[0]
    tm = min(512, n)
    out = jax.ShapeDtypeStruct((n, NA_WIDTH), BF16)
    row = lambda i: (i, 0)
    return pl.pallas_call(
        _in_proj_kernel,
        grid=(n // tm,),
        in_specs=[pl.BlockSpec((tm, D_MODEL), row),
                  _const_spec((1, D_MODEL)),
                  _const_spec((D_MODEL, 4 * NA_WIDTH))],
        out_specs=[pl.BlockSpec((tm, NA_WIDTH), row)] * 4,
        out_shape=[out] * 4,
        compiler_params=_cparams(("parallel",)),
        name="in_proj",
    )(x2d, g_mix.reshape(1, D_MODEL), w_in_bf)


def _na_kernel(q_ref, k_ref, v_ref, b_ref, o_ref, *, rows_per_step, n_rows):
    rb = pl.program_id(2)
    lane = lax.broadcasted_iota(jnp.int32, (1, LANES), 1)
    m0 = jnp.where(lane < NA_HEAD_DIM, 1.0, 0.0).astype(BF16)
    m1 = jnp.where(lane >= NA_HEAD_DIM, 1.0, 0.0).astype(BF16)
    lane_f = lax.broadcasted_iota(jnp.int32, (GRID_W, LANES), 1)
    scale = NA_HEAD_DIM ** -0.5 * LOG2_E

    def score_stage(g):
        out = []
        for j in range(NA_GROUP):
            i = g * NA_GROUP + j
            r = rb * rows_per_step + i
            r0 = jnp.clip(r - WIN_ROWS // 2, 0, n_rows - WIN_ROWS)
            qr = q_ref[0, i * GRID_W:(i + 1) * GRID_W, :]
            qq = jnp.concatenate([qr * m0, qr * m1], axis=0)
            kstart = pl.multiple_of(r0 * GRID_W, GRID_W)
            kw = k_ref[0, pl.ds(kstart, WIN_ROWS * GRID_W), :]
            s = lax.dot_general(qq, kw, (((1,), (1,)), ((), ())), preferred_element_type=F32)
            out.append((i, r - r0, kstart, s))
        return out

    def softmax_stage(rows):
        out = []
        for i, d, kstart, s in rows:
            s = s * scale + b_ref[0, d]
            p = jnp.exp2(s - jnp.max(s, axis=-1, keepdims=True))
            out.append((i, kstart, p.astype(BF16), jnp.sum(p, axis=-1, keepdims=True)))
        return out

    def value_stage(rows):
        for i, kstart, p, l in rows:
            vw = v_ref[0, pl.ds(kstart, WIN_ROWS * GRID_W), :]
            o = jnp.dot(p, vw, preferred_element_type=F32) / l
            out = jnp.where(lane_f < NA_HEAD_DIM, o[:GRID_W], o[GRID_W:])
            o_ref[0, i * GRID_W:(i + 1) * GRID_W, :] = out.astype(BF16)

    n_groups = rows_per_step // NA_GROUP
    scored, soft = None, None
    for g in range(n_groups + 2):
        nxt_scored = score_stage(g) if g < n_groups else None
        nxt_soft = softmax_stage(scored) if scored is not None else None
        if soft is not None:
            value_stage(soft)
        scored, soft = nxt_scored, nxt_soft


def _na_bias_table(rpb):
    cols = jnp.arange(GRID_W)
    c0 = jnp.clip(cols - WIN_COLS // 2, 0, GRID_W - WIN_COLS)
    col_mask = (cols[None, :] >= c0[:, None]) & (cols[None, :] < c0[:, None] + WIN_COLS)
    dc_idx = jnp.clip(cols[None, :] - cols[:, None], -(WIN_COLS - 1), WIN_COLS - 1) + WIN_COLS - 1
    d = jnp.arange(WIN_ROWS)
    dr_idx = -d[:, None] + jnp.arange(WIN_ROWS)[None, :] + WIN_ROWS - 1
    t = (rpb.astype(F32) * LOG2_E)[:, dr_idx]
    t = t[:, :, :, dc_idx]
    t = jnp.where(col_mask[None, None, None], t, NEG_INF)
    t = t.transpose(0, 1, 3, 2, 4).reshape(NA_HEADS, WIN_ROWS, GRID_W, WIN_ROWS * GRID_W)
    t = t.reshape(NA_HEADS // 2, 2, WIN_ROWS, GRID_W, WIN_ROWS * GRID_W).transpose(0, 2, 1, 3, 4)
    return t.reshape(NA_HEADS // 2, WIN_ROWS, 2 * GRID_W, WIN_ROWS * GRID_W)


def _na_attention(q, k, v, bias, b, t):
    n_rows = t // GRID_W
    assert n_rows >= WIN_ROWS
    rows_per_step = min(NA_ROWS, n_rows)
    assert rows_per_step % NA_GROUP == 0 and n_rows % rows_per_step == 0
    tq = rows_per_step * GRID_W
    q3, k3, v3 = (a.reshape(b, t, NA_WIDTH) for a in (q, k, v))
    n_pairs = NA_HEADS // 2
    kern = functools.partial(_na_kernel, rows_per_step=rows_per_step, n_rows=n_rows)
    out = pl.pallas_call(
        kern,
        grid=(b, n_pairs, n_rows // rows_per_step),
        in_specs=[pl.BlockSpec((1, tq, LANES), lambda bi, p, r: (bi, r, p)),
                  pl.BlockSpec((1, t, LANES), lambda bi, p, r: (bi, 0, p)),
                  pl.BlockSpec((1, t, LANES), lambda bi, p, r: (bi, 0, p)),
                  pl.BlockSpec((1, WIN_ROWS, 2 * GRID_W, WIN_ROWS * GRID_W),
                               lambda bi, p, r: (p, 0, 0, 0))],
        out_specs=pl.BlockSpec((1, tq, LANES), lambda bi, p, r: (bi, r, p)),
        out_shape=jax.ShapeDtypeStruct((b, t, NA_WIDTH), BF16),
        compiler_params=_cparams(("parallel", "parallel", "parallel")),
        name="na_attention",
    )(q3, k3, v3, bias)
    return out.reshape(b * t, NA_WIDTH)


def _mem_kv_kernel(m_ref, g_ref, wk_ref, wv_ref, k_ref, v_ref):
    h = _rms(m_ref[0], g_ref[...]).astype(BF16)
    k_ref[0] = jnp.dot(h, wk_ref[...], preferred_element_type=F32).astype(BF16)
    v_ref[0] = jnp.dot(h, wv_ref[...], preferred_element_type=F32).astype(BF16)


def _mem_kv(mem, g_mkv, w_mk_bf, w_mv_bf):
    b, m, _ = mem.shape
    out = jax.ShapeDtypeStruct((b, m, MEM_WIDTH), BF16)
    return pl.pallas_call(
        _mem_kv_kernel,
        grid=(b,),
        in_specs=[pl.BlockSpec((1, m, D_MODEL), lambda i: (i, 0, 0)),
                  _const_spec((1, D_MODEL)),
                  _const_spec((D_MODEL, MEM_WIDTH)),
                  _const_spec((D_MODEL, MEM_WIDTH))],
        out_specs=[pl.BlockSpec((1, m, MEM_WIDTH), lambda i: (i, 0, 0))] * 2,
        out_shape=[out] * 2,
        compiler_params=_cparams(("parallel",)),
        name="mem_kv",
    )(mem, g_mkv.reshape(1, D_MODEL), w_mk_bf, w_mv_bf)


def _mix_kernel(x_ref, ona_ref, up_ref, uc_ref, un_ref, mk_ref, mv_ref,
                wpool_ref, pscale_ref, wout_ref, gmq_ref, wmq_ref, wmo_ref, gffn_ref,
                wr_ref, xh_ref, aff_ref, *, tm, seq):
    t0 = pl.program_id(1) * tm
    pm = tm // MIX_PARTS
    width = pm + 2 * POOL_HALO
    ucat = jnp.concatenate([up_ref[0], uc_ref[0], un_ref[0]], axis=0)
    mk = mk_ref[0]
    mv = mv_ref[0]
    lane = lax.broadcasted_iota(jnp.int32, (pm, LANES), 1)
    parts = [slice(j * pm, (j + 1) * pm) for j in range(MIX_PARTS)]

    def pool(rows):
        p0 = t0 + rows.start
        useg = ucat[rows.start:rows.start + width]
        pos = p0 + lax.broadcasted_iota(jnp.int32, (pm, width), 0)
        src = p0 - POOL_HALO + lax.broadcasted_iota(jnp.int32, (pm, width), 1)
        pcol = p0 + lax.broadcasted_iota(jnp.int32, (pm, 1), 0)
        pools = []
        for g, w in enumerate(POOL_WINDOWS):
            sl = slice(g * POOL_GROUP, (g + 1) * POOL_GROUP)
            lo = jnp.maximum(pos - w // 2, 0)
            hi = jnp.minimum(pos + w // 2, seq)
            band = jnp.where(src >= lo, jnp.where(src < hi, 1.0, 0.0), 0.0).astype(BF16)
            wsum = jnp.dot(band, useg[:, sl], preferred_element_type=F32)
            cnt = (jnp.minimum(pcol + w // 2, seq) - jnp.maximum(pcol - w // 2, 0)).astype(F32)
            dlt = (wsum / cnt - useg[POOL_HALO:POOL_HALO + pm, sl].astype(F32)).astype(BF16)
            pools.append(jnp.dot(dlt, wpool_ref[g], preferred_element_type=F32))
        return (jnp.concatenate(pools, axis=1) * pscale_ref[...]).astype(BF16)

    def mix(rows, o_pool):
        return (x_ref[0, rows, :]
                + jnp.dot(ona_ref[0, rows, :], wout_ref[:NA_WIDTH, :], preferred_element_type=F32)
                + jnp.dot(o_pool, wout_ref[NA_WIDTH:, :], preferred_element_type=F32))

    def mem_query(x1):
        hq = _rms(x1, gmq_ref[...]).astype(BF16)
        return jnp.dot(hq, wmq_ref[...], preferred_element_type=F32).astype(BF16)

    def mem_attend(q):
        heads = []
        for h in range(MEM_HEADS):
            sl = slice(h * MEM_HEAD_DIM, (h + 1) * MEM_HEAD_DIM)
            s = lax.dot_general(q[:, sl], mk[:, sl], (((1,), (1,)), ((), ())),
                                preferred_element_type=F32) * (MEM_HEAD_DIM ** -0.5)
            p = jnp.exp(s - jnp.max(s, axis=-1, keepdims=True))
            l = jnp.sum(p, axis=-1, keepdims=True)
            heads.append(jnp.dot(p.astype(BF16), mv[:, sl], preferred_element_type=F32) / l)
        return jnp.concatenate(heads, axis=1).astype(BF16)

    def mem_out(x1, o_mem):
        return x1 + jnp.dot(o_mem, wmo_ref[...], preferred_element_type=F32)

    def route(rows, x2):
        h3 = _rms(x2, gffn_ref[...])
        h_hi = h3.astype(BF16)
        h_lo = (h3 - h_hi.astype(F32)).astype(BF16)
        both = jnp.dot(h_hi, wr_ref[...], preferred_element_type=F32)
        logits = (both[:, :LANES] + both[:, LANES:]
                  + jnp.dot(h_lo, wr_ref[:, :LANES], preferred_element_type=F32))
        logits = jnp.where(lane < N_EXPERTS, logits, NEG_INF)
        ex = jnp.exp(logits - jnp.max(logits, axis=-1, keepdims=True))
        aff = ex / jnp.sum(ex, axis=-1, keepdims=True)
        xh_ref[rows, 0:D_MODEL] = x2
        xh_ref[rows, D_MODEL:2 * D_MODEL] = h3
        xh_ref[rows, 2 * D_MODEL:] = aff
        aff_ref[rows, :] = aff[:, :N_EXPERTS]

    o_pool = [pool(r) for r in parts]
    x1 = [mix(r, o) for r, o in zip(parts, o_pool)]
    q = [mem_query(v) for v in x1]
    o_mem = [mem_attend(v) for v in q]
    x2 = [mem_out(a, o) for a, o in zip(x1, o_mem)]
    for r, v in zip(parts, x2):
        route(r, v)


def _mix(x, ona, u, mk, mv, wts, b, t):
    tm = min(256, t)
    nt = t // tm
    hb = tm // POOL_HALO
    last_hb = t // POOL_HALO - 1
    x3 = x
    ona3 = ona.reshape(b, t, NA_WIDTH)
    u3 = u.reshape(b, t, POOL_WIDTH)
    kern = functools.partial(_mix_kernel, tm=tm, seq=t)
    n = b * t
    return pl.pallas_call(
        kern,
        grid=(b, nt),
        in_specs=[
            pl.BlockSpec((1, tm, D_MODEL), lambda bi, i: (bi, i, 0)),
            pl.BlockSpec((1, tm, NA_WIDTH), lambda bi, i: (bi, i, 0)),
            pl.BlockSpec((1, POOL_HALO, POOL_WIDTH), lambda bi, i: (bi, jnp.maximum(i * hb - 1, 0), 0)),
            pl.BlockSpec((1, tm, POOL_WIDTH), lambda bi, i: (bi, i, 0)),
            pl.BlockSpec((1, POOL_HALO, POOL_WIDTH),
                         lambda bi, i: (bi, jnp.minimum((i + 1) * hb, last_hb), 0)),
            pl.BlockSpec((1, mk.shape[1], MEM_WIDTH), lambda bi, i: (bi, 0, 0)),
            pl.BlockSpec((1, mk.shape[1], MEM_WIDTH), lambda bi, i: (bi, 0, 0)),
            _const_spec((len(POOL_WINDOWS), POOL_GROUP, POOL_GROUP)),
            _const_spec((1, POOL_WIDTH)),
            _const_spec((D_MODEL, D_MODEL)),
            _const_spec((1, D_MODEL)),
            _const_spec((D_MODEL, MEM_WIDTH)),
            _const_spec((MEM_WIDTH, D_MODEL)),
            _const_spec((1, D_MODEL)),
            _const_spec((D_MODEL, 2 * LANES)),
        ],
        out_specs=[pl.BlockSpec((tm, XH_WIDTH), lambda bi, i: (bi * nt + i, 0)),
                   pl.BlockSpec((tm, N_EXPERTS), lambda bi, i: (bi * nt + i, 0))],
        out_shape=[jax.ShapeDtypeStruct((n, XH_WIDTH), F32),
                   jax.ShapeDtypeStruct((n, N_EXPERTS), F32)],
        compiler_params=_cparams(("parallel", "parallel")),
        name="mix_mem_router",
    )(x3, ona3, u3, u3, u3, mk, mv, *wts)


def _route_kernel(a_ref, idx_ref, *, nch, cap):
    a = a_ref[0]
    capf = jnp.float32(cap)

    def total(v):
        return jnp.sum(jnp.sum(v, axis=1, keepdims=True), axis=0, keepdims=True)

    def as_float(bits):
        return pltpu.bitcast(bits, F32)[0:1, 0:1]

    def search(b, lo):
        cand = lo | jnp.left_shift(jnp.int32(1), 30 - b)
        cnt = total(jnp.where(a >= as_float(cand), 1.0, 0.0))
        return jnp.where(cnt >= capf, cand, lo)

    thr_bits = lax.fori_loop(0, 31, search, jnp.zeros((8, LANES), jnp.int32))
    thr = as_float(thr_bits)
    gt = a >= as_float(thr_bits + 1)
    eq = jnp.logical_and(a >= thr, jnp.logical_not(gt))
    need = capf - total(jnp.where(gt, 1.0, 0.0))

    kr = lax.broadcasted_iota(jnp.int32, (LANES, LANES), 0)
    kc = lax.broadcasted_iota(jnp.int32, (LANES, LANES), 1)
    tri_lane = jnp.where(kr <= kc, 1.0, 0.0).astype(BF16)
    cr = lax.broadcasted_iota(jnp.int32, (nch, nch), 0)
    cc = lax.broadcasted_iota(jnp.int32, (nch, nch), 1)
    tri_excl = jnp.where(cc < cr, 1.0, 0.0).astype(BF16)
    tri_incl = jnp.where(cc <= cr, 1.0, 0.0).astype(BF16)

    def prefix(maskf, chunk_tri):
        within = jnp.dot(maskf.astype(BF16), tri_lane, preferred_element_type=F32)
        tot = jnp.broadcast_to(within[:, LANES - 1:LANES], (nch, LANES))
        before = jnp.dot(chunk_tri, tot.astype(BF16), preferred_element_type=F32)
        return within, before

    eqf = jnp.where(eq, 1.0, 0.0)
    eq_within, eq_before = prefix(eqf, tri_excl)
    eq_rank = eq_before + eq_within - eqf
    self_ = jnp.where(gt, 1.0, jnp.where(eq, jnp.where(eq_rank < need, 1.0, 0.0), 0.0))

    within, cs_incl = prefix(self_, tri_incl)
    cs_col = cs_incl[:, 0:1]
    j = lax.broadcasted_iota(jnp.int32, (1, cap), 1).astype(F32)
    done = cs_col <= j
    chunk = jnp.sum(jnp.where(done, 1.0, 0.0), axis=0, keepdims=True)
    cs_prev = jnp.max(jnp.where(done, cs_col, 0.0), axis=0, keepdims=True)
    crow = lax.broadcasted_iota(jnp.int32, (nch, cap), 0).astype(F32)
    onehot = jnp.where(crow == chunk, 1.0, 0.0).astype(BF16)
    wsel = jnp.dot(within.T.astype(BF16), onehot, preferred_element_type=F32)
    rank = j - cs_prev
    inner = jnp.sum(jnp.where(wsel <= rank, 1.0, 0.0), axis=0, keepdims=True)
    idx_ref[0] = (chunk * LANES + inner).astype(jnp.int32)


def _route(aff, n):
    cap = EC_CAPACITY_FACTOR * n // N_EXPERTS
    nch = n // LANES
    aff_t = aff.T.reshape(N_EXPERTS, nch, LANES)
    kern = functools.partial(_route_kernel, nch=nch, cap=cap)
    idx = pl.pallas_call(
        kern,
        grid=(N_EXPERTS,),
        in_specs=[pl.BlockSpec((1, nch, LANES), lambda e: (e, 0, 0))],
        out_specs=pl.BlockSpec((1, 1, cap), lambda e: (e, 0, 0)),
        out_shape=jax.ShapeDtypeStruct((N_EXPERTS, 1, cap), jnp.int32),
        compiler_params=_cparams(("parallel",)),
        name="route",
    )(aff_t)
    return idx.reshape(N_EXPERTS * cap), cap


def _moe_kernel(idx_ref, xh_in_ref, wg_ref, wu_ref, wd_ref, xh_ref, xbuf, obuf, hid_ref, gsem, ssem,
                *, tm, tiles):
    del xh_in_ref
    e = pl.program_id(0)
    i = pl.program_id(1)
    s = e * tiles + i
    last = N_EXPERTS * tiles - 1
    slot = lax.rem(s, 2)
    other = 1 - slot
    base = s * tm

    def gather_copy(tok, r, sl):
        return pltpu.make_async_copy(xh_ref.at[pl.ds(tok, 1), :], xbuf.at[sl, pl.ds(r, 1), :], gsem.at[sl])

    def scatter_copy(tok, r, sl):
        return pltpu.make_async_copy(obuf.at[sl, pl.ds(r, 1), :],
                                     xh_ref.at[pl.ds(tok, 1), pl.ds(0, D_MODEL)], ssem.at[sl])

    def start_rows(copy, first, sl):
        def body(r, c):
            copy(idx_ref[first + r], r, sl).start()
            return c
        lax.fori_loop(0, tm, body, 0, unroll=8)

    def wait_rows(copy, sl):
        for _ in range(tm):
            copy(0, 0, sl).wait()

    @pl.when(s >= 1)
    def _():
        wait_rows(scatter_copy, slot)

    @pl.when(s == 0)
    def _():
        start_rows(gather_copy, base, slot)

    @pl.when((i == 0) & (s >= 1))
    def _():
        wait_rows(gather_copy, slot)
        start_rows(scatter_copy, base - tm, other)
        wait_rows(scatter_copy, other)
        start_rows(gather_copy, base, slot)

    wait_rows(gather_copy, slot)

    @pl.when(i == 0)
    def _():
        obuf[other] = xbuf[slot, :, 0:D_MODEL]

    wb_first = jnp.where(i == 0, base, base - tm)
    pf_first = jnp.where(i + 1 < tiles, base + tm, base)
    rows_per_chunk = tm // MOE_CHUNKS
    fw = D_MODEL // MOE_CHUNKS
    xs = xbuf[slot, :, D_MODEL:2 * D_MODEL].astype(BF16)
    for c in range(MOE_CHUNKS):
        a = jnp.dot(xs, wg_ref[0, :, c * fw:(c + 1) * fw], preferred_element_type=F32)
        up = jnp.dot(xs, wu_ref[0, :, c * fw:(c + 1) * fw], preferred_element_type=F32)
        hid_ref[:, c * fw:(c + 1) * fw] = (a * jax.nn.sigmoid(a) * up).astype(BF16)
        for r in range(c * rows_per_chunk, (c + 1) * rows_per_chunk):
            scatter_copy(idx_ref[wb_first + r], r, other).start()
            gather_copy(idx_ref[pf_first + r], r, other).start()
    y = jnp.dot(hid_ref[...], wd_ref[0], preferred_element_type=F32)
    lane = lax.broadcasted_iota(jnp.int32, (tm, LANES), 1)
    gate = jnp.sum(jnp.where(lane == e, xbuf[slot, :, 2 * D_MODEL:], 0.0), axis=1, keepdims=True)
    obuf[slot] = xbuf[slot, :, 0:D_MODEL] + y * gate

    @pl.when(s == last)
    def _():
        wait_rows(gather_copy, other)
        wait_rows(scatter_copy, other)
        start_rows(scatter_copy, base, slot)
        wait_rows(scatter_copy, slot)


def _moe(idx, xh, wg_bf, wu_bf, wd_bf, cap, tm):
    tiles = cap // tm
    assert tiles >= 2 and tm % MOE_CHUNKS == 0
    n = xh.shape[0]
    kern = functools.partial(_moe_kernel, tm=tm, tiles=tiles)
    wspec = pl.BlockSpec((1, D_MODEL, D_MODEL), lambda e, i, idx_ref: (e, 0, 0), pipeline_mode=pl.Buffered(1))
    return pl.pallas_call(
        kern,
        grid_spec=pltpu.PrefetchScalarGridSpec(
            num_scalar_prefetch=1,
            grid=(N_EXPERTS, tiles),
            in_specs=[pl.BlockSpec(memory_space=pl.ANY), wspec, wspec, wspec],
            out_specs=pl.BlockSpec(memory_space=pl.ANY),
            scratch_shapes=[pltpu.VMEM((2, tm, XH_WIDTH), F32),
                            pltpu.VMEM((2, tm, D_MODEL), F32),
                            pltpu.VMEM((tm, D_MODEL), BF16),
                            pltpu.SemaphoreType.DMA((2,)),
                            pltpu.SemaphoreType.DMA((2,))],
        ),
        out_shape=jax.ShapeDtypeStruct((n, XH_WIDTH), F32),
        input_output_aliases={1: 0},
        compiler_params=pltpu.CompilerParams(dimension_semantics=("arbitrary", "arbitrary"),
                                             vmem_limit_bytes=VMEM_LIMIT, disable_bounds_checks=True),
        name="moe_ffn",
    )(idx, xh, wg_bf, wu_bf, wd_bf)


def _final_kernel(x_ref, g_ref, o_ref):
    o_ref[...] = _rms(x_ref[...], g_ref[...])


def _final_norm(xh, g_final):
    n = xh.shape[0]
    tm = min(512, n)
    return pl.pallas_call(
        _final_kernel,
        grid=(n // tm,),
        in_specs=[pl.BlockSpec((tm, D_MODEL), lambda i: (i, 0)), _const_spec((1, D_MODEL))],
        out_specs=pl.BlockSpec((tm, D_MODEL), lambda i: (i, 0)),
        out_shape=jax.ShapeDtypeStruct((n, D_MODEL), F32),
        compiler_params=_cparams(("parallel",)),
        name="final_norm",
    )(xh, g_final.reshape(1, D_MODEL))


def _trunk(x, mem, p, moe_tm=256):
    b, t, _ = x.shape
    n = b * t
    q, k, v, u = _in_proj(x.reshape(n, D_MODEL), p["g_mix"], p["w_in"])
    ona = _na_attention(q, k, v, p["na_bias"], b, t)
    mk, mv = _mem_kv(mem, p["g_mkv"], p["w_mk"], p["w_mv"])
    wts = (p["w_pool"], p["pool_scale"], p["w_out"], p["g_mq"], p["w_mq"], p["w_mo"], p["g_ffn"],
           p["w_router"])
    xh, aff = _mix(x, ona, u, mk, mv, wts, b, t)
    idx, cap = _route(aff, n)
    xh = _moe(idx, xh, p["w_gate"], p["w_up"], p["w_down"], cap, min(moe_tm, cap))
    return _final_norm(xh, p["g_final"]).reshape(b, t, D_MODEL)


def _prepare(g_mix, w_in, rpb, w_pool, pool_scale, w_out, g_mq, g_mkv, w_mq, w_mk, w_mv, w_mo,
             g_ffn, w_router, w_gate, w_up, w_down, g_final):
    wr = jnp.pad(w_router[0], ((0, 0), (0, LANES - N_EXPERTS)))
    wr_hi = wr.astype(BF16)
    wr_lo = (wr - wr_hi.astype(F32)).astype(BF16)
    row = lambda a: a.reshape(1, -1)
    return dict(
        g_mix=g_mix[0], w_in=w_in[0].astype(BF16), na_bias=_na_bias_table(rpb[0]),
        w_pool=w_pool[0].astype(BF16), pool_scale=row(pool_scale[0]), w_out=w_out[0].astype(BF16),
        g_mq=row(g_mq[0]), g_mkv=g_mkv[0], w_mq=w_mq[0].astype(BF16), w_mk=w_mk[0].astype(BF16),
        w_mv=w_mv[0].astype(BF16), w_mo=w_mo[0].astype(BF16), g_ffn=row(g_ffn[0]),
        w_router=jnp.concatenate([wr_hi, wr_lo], axis=1),
        w_gate=w_gate[0].astype(BF16), w_up=w_up[0].astype(BF16), w_down=w_down[0].astype(BF16),
        g_final=g_final)


def kernel(x_prompt, x_sample, mem_prompt, mem_sample, g_mix, w_in, rpb, w_pool, pool_scale, w_out,
           g_mq, g_mkv, w_mq, w_mk, w_mv, w_mo, g_ffn, w_router, w_gate, w_up, w_down, g_final):
    p = _prepare(g_mix, w_in, rpb, w_pool, pool_scale, w_out, g_mq, g_mkv, w_mq, w_mk, w_mv, w_mo,
                 g_ffn, w_router, w_gate, w_up, w_down, g_final)
    return _trunk(x_prompt, mem_prompt, p), _trunk(x_sample, mem_sample, p)
```

```python
import functools

import jax
import jax.numpy as jnp
from jax import lax
from jax.experimental import pallas as pl
from jax.experimental.pallas import tpu as pltpu

F32 = jnp.float32
BF16 = jnp.bfloat16

D_MODEL = 2048
GRID_W = 64
NA_WIDTH = 1024
NA_HEAD_DIM = 64
NA_HEADS = 16
WIN_ROWS = 8
WIN_COLS = 16
POOL_WINDOWS = (2, 4, 8, 16)
POOL_WIDTH = 1024
POOL_GROUP = 256
MEM_HEADS = 4
MEM_HEAD_DIM = 128
MEM_WIDTH = 512
N_EXPERTS = 16
EC_CAPACITY_FACTOR = 2
EPS = 1e-6
NEG_INF = -1e30
LOG2_E = 1.4426950408889634

LANES = 128
POOL_HALO = 16
XH_WIDTH = 2 * D_MODEL + LANES
VMEM_LIMIT = 56 * 1024 * 1024
MOE_CHUNKS = 8
NA_ROWS = 32
NA_GROUP = 2
MIX_PARTS = 2


def _cparams(sem, vmem=VMEM_LIMIT):
    return pltpu.CompilerParams(dimension_semantics=sem, vmem_limit_bytes=vmem)


def _const_spec(shape):
    nd = len(shape)
    return pl.BlockSpec(shape, lambda *_: (0,) * nd, pipeline_mode=pl.Buffered(1))


def _rms(v, g):
    return v * lax.rsqrt(jnp.mean(v * v, axis=-1, keepdims=True) + EPS) * g


def _pick(step, starts, refs, read):
    val = read(refs[0])
    for start, ref in zip(starts[1:], refs[1:]):
        val = jnp.where(step >= start, read(ref), val)
    return val


def _held_index(step, start, count):
    return jnp.clip(step - start, 0, count - 1)


def _in_proj_kernel(*refs, starts):
    n_src = len(starts)
    x_refs, (g_ref, w_ref), outs = refs[:n_src], refs[n_src:n_src + 2], refs[n_src + 2:]
    x = _pick(pl.program_id(0), starts, x_refs, lambda r: r[...])
    h = _rms(x, g_ref[...]).astype(BF16)
    for j, o_ref in enumerate(outs):
        o_ref[...] = jnp.dot(h, w_ref[:, j * NA_WIDTH:(j + 1) * NA_WIDTH],
                             preferred_element_type=F32).astype(BF16)


def _in_proj(x2ds, g_mix, w_in_bf):
    tm = min(512, *(x.shape[0] for x in x2ds))
    counts = [x.shape[0] // tm for x in x2ds]
    starts = tuple(sum(counts[:j]) for j in range(len(x2ds)))
    n = sum(x.shape[0] for x in x2ds)
    out = jax.ShapeDtypeStruct((n, NA_WIDTH), BF16)
    row = lambda i: (i, 0)
    x_specs = [pl.BlockSpec((tm, D_MODEL), functools.partial(lambda i, s, c: (_held_index(i, s, c), 0), s=s, c=c))
               for s, c in zip(starts, counts)]
    return pl.pallas_call(
        functools.partial(_in_proj_kernel, starts=starts),
        grid=(n // tm,),
        in_specs=x_specs + [_const_spec((1, D_MODEL)), _const_spec((D_MODEL, 4 * NA_WIDTH))],
        out_specs=[pl.BlockSpec((tm, NA_WIDTH), row)] * 4,
        out_shape=[out] * 4,
        compiler_params=_cparams(("parallel",)),
        name="in_proj",
    )(*x2ds, g_mix.reshape(1, D_MODEL), w_in_bf)


def _na_kernel(q_ref, k_ref, v_ref, b_ref, o_ref, *, rows_per_step, n_rows):
    rb = pl.program_id(2)
    lane = lax.broadcasted_iota(jnp.int32, (1, LANES), 1)
    m0 = jnp.where(lane < NA_HEAD_DIM, 1.0, 0.0).astype(BF16)
    m1 = jnp.where(lane >= NA_HEAD_DIM, 1.0, 0.0).astype(BF16)
    lane_f = lax.broadcasted_iota(jnp.int32, (GRID_W, LANES), 1)
    scale = NA_HEAD_DIM ** -0.5 * LOG2_E

    def score_stage(g):
        out = []
        for j in range(NA_GROUP):
            i = g * NA_GROUP + j
            r = rb * rows_per_step + i
            r0 = jnp.clip(r - WIN_ROWS // 2, 0, n_rows - WIN_ROWS)
            qr = q_ref[0, i * GRID_W:(i + 1) * GRID_W, :]
            qq = jnp.concatenate([qr * m0, qr * m1], axis=0)
            kstart = pl.multiple_of(r0 * GRID_W, GRID_W)
            kw = k_ref[0, pl.ds(kstart, WIN_ROWS * GRID_W), :]
            s = lax.dot_general(qq, kw, (((1,), (1,)), ((), ())), preferred_element_type=F32)
            out.append((i, r - r0, kstart, s))
        return out

    def softmax_stage(rows):
        out = []
        for i, d, kstart, s in rows:
            s = s * scale + b_ref[0, d]
            p = jnp.exp2(s - jnp.max(s, axis=-1, keepdims=True))
            out.append((i, kstart, p.astype(BF16), jnp.sum(p, axis=-1, keepdims=True)))
        return out

    def value_stage(rows):
        for i, kstart, p, l in rows:
            vw = v_ref[0, pl.ds(kstart, WIN_ROWS * GRID_W), :]
            o = jnp.dot(p, vw, preferred_element_type=F32) / l
            out = jnp.where(lane_f < NA_HEAD_DIM, o[:GRID_W], o[GRID_W:])
            o_ref[0, i * GRID_W:(i + 1) * GRID_W, :] = out.astype(BF16)

    n_groups = rows_per_step // NA_GROUP
    scored, soft = None, None
    for g in range(n_groups + 2):
        nxt_scored = score_stage(g) if g < n_groups else None
        nxt_soft = softmax_stage(scored) if scored is not None else None
        if soft is not None:
            value_stage(soft)
        scored, soft = nxt_scored, nxt_soft


def _na_bias_table(rpb):
    cols = jnp.arange(GRID_W)
    c0 = jnp.clip(cols - WIN_COLS // 2, 0, GRID_W - WIN_COLS)
    col_mask = (cols[None, :] >= c0[:, None]) & (cols[None, :] < c0[:, None] + WIN_COLS)
    dc_idx = jnp.clip(cols[None, :] - cols[:, None], -(WIN_COLS - 1), WIN_COLS - 1) + WIN_COLS - 1
    d = jnp.arange(WIN_ROWS)
    dr_idx = -d[:, None] + jnp.arange(WIN_ROWS)[None, :] + WIN_ROWS - 1
    t = (rpb.astype(F32) * LOG2_E)[:, dr_idx]
    t = t[:, :, :, dc_idx]
    t = jnp.where(col_mask[None, None, None], t, NEG_INF)
    t = t.transpose(0, 1, 3, 2, 4).reshape(NA_HEADS, WIN_ROWS, GRID_W, WIN_ROWS * GRID_W)
    t = t.reshape(NA_HEADS // 2, 2, WIN_ROWS, GRID_W, WIN_ROWS * GRID_W).transpose(0, 2, 1, 3, 4)
    return t.reshape(NA_HEADS // 2, WIN_ROWS, 2 * GRID_W, WIN_ROWS * GRID_W)


def _na_attention(q, k, v, bias, b, t):
    n_rows = t // GRID_W
    assert n_rows >= WIN_ROWS
    rows_per_step = min(NA_ROWS, n_rows)
    assert rows_per_step % NA_GROUP == 0 and n_rows % rows_per_step == 0
    tq = rows_per_step * GRID_W
    q3, k3, v3 = (a.reshape(b, t, NA_WIDTH) for a in (q, k, v))
    n_pairs = NA_HEADS // 2
    kern = functools.partial(_na_kernel, rows_per_step=rows_per_step, n_rows=n_rows)
    out = pl.pallas_call(
        kern,
        grid=(b, n_pairs, n_rows // rows_per_step),
        in_specs=[pl.BlockSpec((1, tq, LANES), lambda bi, p, r: (bi, r, p)),
                  pl.BlockSpec((1, t, LANES), lambda bi, p, r: (bi, 0, p)),
                  pl.BlockSpec((1, t, LANES), lambda bi, p, r: (bi, 0, p)),
                  pl.BlockSpec((1, WIN_ROWS, 2 * GRID_W, WIN_ROWS * GRID_W),
                               lambda bi, p, r: (p, 0, 0, 0))],
        out_specs=pl.BlockSpec((1, tq, LANES), lambda bi, p, r: (bi, r, p)),
        out_shape=jax.ShapeDtypeStruct((b, t, NA_WIDTH), BF16),
        compiler_params=_cparams(("parallel", "parallel", "parallel")),
        name="na_attention",
    )(q3, k3, v3, bias)
    return out.reshape(b * t, NA_WIDTH)


def _mem_kv_kernel(m_ref, g_ref, wk_ref, wv_ref, k_ref, v_ref):
    h = _rms(m_ref[0], g_ref[...]).astype(BF16)
    k_ref[0] = jnp.dot(h, wk_ref[...], preferred_element_type=F32).astype(BF16)
    v_ref[0] = jnp.dot(h, wv_ref[...], preferred_element_type=F32).astype(BF16)


def _mem_kv(mem, g_mkv, w_mk_bf, w_mv_bf):
    b, m, _ = mem.shape
    out = jax.ShapeDtypeStruct((b, m, MEM_WIDTH), BF16)
    return pl.pallas_call(
        _mem_kv_kernel,
        grid=(b,),
        in_specs=[pl.BlockSpec((1, m, D_MODEL), lambda i: (i, 0, 0)),
                  _const_spec((1, D_MODEL)),
                  _const_spec((D_MODEL, MEM_WIDTH)),
                  _const_spec((D_MODEL, MEM_WIDTH))],
        out_specs=[pl.BlockSpec((1, m, MEM_WIDTH), lambda i: (i, 0, 0))] * 2,
        out_shape=[out] * 2,
        compiler_params=_cparams(("parallel",)),
        name="mem_kv",
    )(mem, g_mkv.reshape(1, D_MODEL), w_mk_bf, w_mv_bf)


def _mix_kernel(*refs, tm, seq, batch_starts):
    x_refs = refs[:len(batch_starts)]
    (ona_ref, up_ref, uc_ref, un_ref, mk_ref, mv_ref, wpool_ref, pscale_ref, wout_ref, gmq_ref,
     wmq_ref, wmo_ref, gffn_ref, wr_ref, xh_ref, afft_ref) = refs[len(batch_starts):]
    t0 = pl.program_id(1) * tm
    pm = tm // MIX_PARTS
    width = pm + 2 * POOL_HALO
    ucat = jnp.concatenate([up_ref[0], uc_ref[0], un_ref[0]], axis=0)
    mk = mk_ref[0]
    mv = mv_ref[0]
    lane = lax.broadcasted_iota(jnp.int32, (pm, LANES), 1)
    parts = [slice(j * pm, (j + 1) * pm) for j in range(MIX_PARTS)]

    def pool(rows):
        p0 = t0 + rows.start
        useg = ucat[rows.start:rows.start + width]
        pos = p0 + lax.broadcasted_iota(jnp.int32, (pm, width), 0)
        src = p0 - POOL_HALO + lax.broadcasted_iota(jnp.int32, (pm, width), 1)
        pcol = p0 + lax.broadcasted_iota(jnp.int32, (pm, 1), 0)
        pools = []
        for g, w in enumerate(POOL_WINDOWS):
            sl = slice(g * POOL_GROUP, (g + 1) * POOL_GROUP)
            lo = jnp.maximum(pos - w // 2, 0)
            hi = jnp.minimum(pos + w // 2, seq)
            band = jnp.where(src >= lo, jnp.where(src < hi, 1.0, 0.0), 0.0).astype(BF16)
            wsum = jnp.dot(band, useg[:, sl], preferred_element_type=F32)
            cnt = (jnp.minimum(pcol + w // 2, seq) - jnp.maximum(pcol - w // 2, 0)).astype(F32)
            dlt = (wsum / cnt - useg[POOL_HALO:POOL_HALO + pm, sl].astype(F32)).astype(BF16)
            pools.append(jnp.dot(dlt, wpool_ref[g], preferred_element_type=F32))
        return (jnp.concatenate(pools, axis=1) * pscale_ref[...]).astype(BF16)

    def mix(rows, o_pool):
        x = _pick(pl.program_id(0), batch_starts, x_refs, lambda r: r[0, rows, :])
        return (x
                + jnp.dot(ona_ref[0, rows, :], wout_ref[:NA_WIDTH, :], preferred_element_type=F32)
                + jnp.dot(o_pool, wout_ref[NA_WIDTH:, :], preferred_element_type=F32))

    def mem_query(x1):
        hq = _rms(x1, gmq_ref[...]).astype(BF16)
        return jnp.dot(hq, wmq_ref[...], preferred_element_type=F32).astype(BF16)

    def mem_attend(q):
        heads = []
        for h in range(MEM_HEADS):
            sl = slice(h * MEM_HEAD_DIM, (h + 1) * MEM_HEAD_DIM)
            s = lax.dot_general(q[:, sl], mk[:, sl], (((1,), (1,)), ((), ())),
                                preferred_element_type=F32) * (MEM_HEAD_DIM ** -0.5)
            p = jnp.exp(s - jnp.max(s, axis=-1, keepdims=True))
            l = jnp.sum(p, axis=-1, keepdims=True)
            heads.append(jnp.dot(p.astype(BF16), mv[:, sl], preferred_element_type=F32) / l)
        return jnp.concatenate(heads, axis=1).astype(BF16)

    def mem_out(x1, o_mem):
        return x1 + jnp.dot(o_mem, wmo_ref[...], preferred_element_type=F32)

    def route(rows, x2):
        h3 = _rms(x2, gffn_ref[...])
        h_hi = h3.astype(BF16)
        h_lo = (h3 - h_hi.astype(F32)).astype(BF16)
        both = jnp.dot(h_hi, wr_ref[...], preferred_element_type=F32)
        logits = (both[:, :LANES] + both[:, LANES:]
                  + jnp.dot(h_lo, wr_ref[:, :LANES], preferred_element_type=F32))
        logits = jnp.where(lane < N_EXPERTS, logits, NEG_INF)
        ex = jnp.exp(logits - jnp.max(logits, axis=-1, keepdims=True))
        aff = ex / jnp.sum(ex, axis=-1, keepdims=True)
        xh_ref[rows, 0:D_MODEL] = x2
        xh_ref[rows, D_MODEL:2 * D_MODEL] = h3
        xh_ref[rows, 2 * D_MODEL:] = aff
        afft_ref[:, rows] = aff.T[:N_EXPERTS, :]

    o_pool = [pool(r) for r in parts]
    x1 = [mix(r, o) for r, o in zip(parts, o_pool)]
    q = [mem_query(v) for v in x1]
    o_mem = [mem_attend(v) for v in q]
    x2 = [mem_out(a, o) for a, o in zip(x1, o_mem)]
    for r, v in zip(parts, x2):
        route(r, v)


def _mix(xs, ona, u, mk, mv, wts, t):
    tm = min(256, t)
    nt = t // tm
    hb = tm // POOL_HALO
    last_hb = t // POOL_HALO - 1
    batches = [x.shape[0] for x in xs]
    batch_starts = tuple(sum(batches[:j]) for j in range(len(xs)))
    b = sum(batches)
    n = b * t
    ona3 = ona.reshape(b, t, NA_WIDTH)
    u3 = u.reshape(b, t, POOL_WIDTH)

    def x_spec(start, count):
        def index(bi, i):
            tile = jnp.where(bi < start, 0, jnp.where(bi >= start + count, nt - 1, i))
            return (_held_index(bi, start, count), tile, 0)
        return pl.BlockSpec((1, tm, D_MODEL), index)

    kern = functools.partial(_mix_kernel, tm=tm, seq=t, batch_starts=batch_starts)
    return pl.pallas_call(
        kern,
        grid=(b, nt),
        in_specs=[x_spec(s, c) for s, c in zip(batch_starts, batches)] + [
            pl.BlockSpec((1, tm, NA_WIDTH), lambda bi, i: (bi, i, 0)),
            pl.BlockSpec((1, POOL_HALO, POOL_WIDTH), lambda bi, i: (bi, jnp.maximum(i * hb - 1, 0), 0)),
            pl.BlockSpec((1, tm, POOL_WIDTH), lambda bi, i: (bi, i, 0)),
            pl.BlockSpec((1, POOL_HALO, POOL_WIDTH),
                         lambda bi, i: (bi, jnp.minimum((i + 1) * hb, last_hb), 0)),
            pl.BlockSpec((1, mk.shape[1], MEM_WIDTH), lambda bi, i: (bi, 0, 0)),
            pl.BlockSpec((1, mk.shape[1], MEM_WIDTH), lambda bi, i: (bi, 0, 0)),
            _const_spec((len(POOL_WINDOWS), POOL_GROUP, POOL_GROUP)),
            _const_spec((1, POOL_WIDTH)),
            _const_spec((D_MODEL, D_MODEL)),
            _const_spec((1, D_MODEL)),
            _const_spec((D_MODEL, MEM_WIDTH)),
            _const_spec((MEM_WIDTH, D_MODEL)),
            _const_spec((1, D_MODEL)),
            _const_spec((D_MODEL, 2 * LANES)),
        ],
        out_specs=[pl.BlockSpec((tm, XH_WIDTH), lambda bi, i: (bi * nt + i, 0)),
                   pl.BlockSpec((N_EXPERTS, tm), lambda bi, i: (0, bi * nt + i))],
        out_shape=[jax.ShapeDtypeStruct((n, XH_WIDTH), F32),
                   jax.ShapeDtypeStruct((N_EXPERTS, n), F32)],
        compiler_params=_cparams(("parallel", "parallel")),
        name="mix_mem_router",
    )(*xs, ona3, u3, u3, u3, mk, mv, *wts)


def _route_kernel(a_ref, idx_ref, *, nch, cap):
    a = a_ref[0]
    capf = jnp.float32(cap)

    def total(v):
        return jnp.sum(jnp.sum(v, axis=1, keepdims=True), axis=0, keepdims=True)

    def as_float(bits):
        return pltpu.bitcast(bits, F32)[0:1, 0:1]

    def search(b, lo):
        cand = lo | jnp.left_shift(jnp.int32(1), 30 - b)
        cnt = total(jnp.where(a >= as_float(cand), 1.0, 0.0))
        return jnp.where(cnt >= capf, cand, lo)

    thr_bits = lax.fori_loop(0, 31, search, jnp.zeros((8, LANES), jnp.int32))
    thr = as_float(thr_bits)
    gt = a >= as_float(thr_bits + 1)
    eq = jnp.logical_and(a >= thr, jnp.logical_not(gt))
    need = capf - total(jnp.where(gt, 1.0, 0.0))

    kr = lax.broadcasted_iota(jnp.int32, (LANES, LANES), 0)
    kc = lax.broadcasted_iota(jnp.int32, (LANES, LANES), 1)
    tri_lane = jnp.where(kr <= kc, 1.0, 0.0).astype(BF16)
    cr = lax.broadcasted_iota(jnp.int32, (nch, nch), 0)
    cc = lax.broadcasted_iota(jnp.int32, (nch, nch), 1)
    tri_excl = jnp.where(cc < cr, 1.0, 0.0).astype(BF16)
    tri_incl = jnp.where(cc <= cr, 1.0, 0.0).astype(BF16)

    def prefix(maskf, chunk_tri):
        within = jnp.dot(maskf.astype(BF16), tri_lane, preferred_element_type=F32)
        tot = jnp.broadcast_to(within[:, LANES - 1:LANES], (nch, LANES))
        before = jnp.dot(chunk_tri, tot.astype(BF16), preferred_element_type=F32)
        return within, before

    eqf = jnp.where(eq, 1.0, 0.0)
    eq_within, eq_before = prefix(eqf, tri_excl)
    eq_rank = eq_before + eq_within - eqf
    self_ = jnp.where(gt, 1.0, jnp.where(eq, jnp.where(eq_rank < need, 1.0, 0.0), 0.0))

    within, cs_incl = prefix(self_, tri_incl)
    cs_col = cs_incl[:, 0:1]
    j = lax.broadcasted_iota(jnp.int32, (1, cap), 1).astype(F32)
    done = cs_col <= j
    chunk = jnp.sum(jnp.where(done, 1.0, 0.0), axis=0, keepdims=True)
    cs_prev = jnp.max(jnp.where(done, cs_col, 0.0), axis=0, keepdims=True)
    crow = lax.broadcasted_iota(jnp.int32, (nch, cap), 0).astype(F32)
    onehot = jnp.where(crow == chunk, 1.0, 0.0).astype(BF16)
    wsel = jnp.dot(within.T.astype(BF16), onehot, preferred_element_type=F32)
    rank = j - cs_prev
    inner = jnp.sum(jnp.where(wsel <= rank, 1.0, 0.0), axis=0, keepdims=True)
    idx_ref[0] = (chunk * LANES + inner).astype(jnp.int32)


def _route(aff_t, row0):
    n = aff_t.shape[1]
    cap = EC_CAPACITY_FACTOR * n // N_EXPERTS
    nch = n // LANES
    kern = functools.partial(_route_kernel, nch=nch, cap=cap)
    idx = pl.pallas_call(
        kern,
        grid=(N_EXPERTS,),
        in_specs=[pl.BlockSpec((1, nch, LANES), lambda e: (e, 0, 0))],
        out_specs=pl.BlockSpec((1, 1, cap), lambda e: (e, 0, 0)),
        out_shape=jax.ShapeDtypeStruct((N_EXPERTS, 1, cap), jnp.int32),
        compiler_params=_cparams(("parallel",)),
        name="route",
    )(aff_t.reshape(N_EXPERTS, nch, LANES))
    return idx.reshape(N_EXPERTS, cap) + row0


def _moe_kernel(idx_ref, xh_in_ref, wg_hbm, wu_hbm, wd_hbm, xh_ref,
                wg_ref, wu_ref, wd_ref, wstage, xbuf, obuf, hid_ref, gsem, ssem, wsem, *, tm, tiles):
    del xh_in_ref
    e = pl.program_id(0)
    i = pl.program_id(1)
    s = e * tiles + i
    slot = lax.rem(s, 2)
    other = 1 - slot
    base = s * tm

    def gather_copy(tok, r, sl):
        return pltpu.make_async_copy(xh_ref.at[pl.ds(tok, 1), :], xbuf.at[sl, pl.ds(r, 1), :], gsem.at[sl])

    def scatter_copy(tok, r, sl):
        return pltpu.make_async_copy(obuf.at[sl, pl.ds(r, 1), :],
                                     xh_ref.at[pl.ds(tok, 1), pl.ds(0, D_MODEL)], ssem.at[sl])

    def start_rows(copy, first, sl):
        def body(r, c):
            copy(idx_ref[first + r], r, sl).start()
            return c
        lax.fori_loop(0, tm, body, 0, unroll=8)

    def wait_rows(copy, sl):
        for _ in range(tm):
            copy(0, 0, sl).wait()

    def load_weights():
        rows = D_MODEL // MOE_CHUNKS
        jobs = [(src, dst, c) for src, dst in ((wg_hbm, wg_ref), (wu_hbm, wu_ref), (wd_hbm, wd_ref))
                for c in range(MOE_CHUNKS)]

        def chunk_copy(k):
            src, _, c = jobs[k]
            return pltpu.make_async_copy(src.at[e, pl.ds(c * rows, rows), :], wstage.at[k % 2], wsem.at[k % 2])

        chunk_copy(0).start()
        for k, (_, dst, c) in enumerate(jobs):
            if k + 1 < len(jobs):
                chunk_copy(k + 1).start()
            chunk_copy(k).wait()
            dst[c * rows:(c + 1) * rows, :] = wstage[k % 2].astype(BF16)

    @pl.when(i >= 1)
    def _():
        wait_rows(scatter_copy, slot)

    @pl.when(i == 0)
    def _():
        start_rows(gather_copy, base, slot)
        load_weights()

    wait_rows(gather_copy, slot)

    @pl.when(i == 0)
    def _():
        obuf[other] = xbuf[slot, :, 0:D_MODEL]

    wb_first = jnp.where(i == 0, base, base - tm)
    pf_first = jnp.where(i + 1 < tiles, base + tm, base)
    rows_per_chunk = tm // MOE_CHUNKS
    fw = D_MODEL // MOE_CHUNKS
    xs = xbuf[slot, :, D_MODEL:2 * D_MODEL].astype(BF16)
    for c in range(MOE_CHUNKS):
        a = jnp.dot(xs, wg_ref[:, c * fw:(c + 1) * fw], preferred_element_type=F32)
        up = jnp.dot(xs, wu_ref[:, c * fw:(c + 1) * fw], preferred_element_type=F32)
        hid_ref[:, c * fw:(c + 1) * fw] = (a * jax.nn.sigmoid(a) * up).astype(BF16)
        for r in range(c * rows_per_chunk, (c + 1) * rows_per_chunk):
            scatter_copy(idx_ref[wb_first + r], r, other).start()
            gather_copy(idx_ref[pf_first + r], r, other).start()
    y = jnp.dot(hid_ref[...], wd_ref[...], preferred_element_type=F32)
    lane = lax.broadcasted_iota(jnp.int32, (tm, LANES), 1)
    gate = jnp.sum(jnp.where(lane == e, xbuf[slot, :, 2 * D_MODEL:], 0.0), axis=1, keepdims=True)
    obuf[slot] = xbuf[slot, :, 0:D_MODEL] + y * gate

    @pl.when(i == tiles - 1)
    def _():
        wait_rows(gather_copy, other)
        wait_rows(scatter_copy, other)
        start_rows(scatter_copy, base, slot)
        wait_rows(scatter_copy, slot)


def _moe(idx, xh, w_gate, w_up, w_down, tm):
    cap = idx.shape[1]
    tiles = cap // tm
    assert cap % tm == 0 and tiles >= 2 and tm % MOE_CHUNKS == 0
    kern = functools.partial(_moe_kernel, tm=tm, tiles=tiles)
    any_spec = pl.BlockSpec(memory_space=pl.ANY)
    return pl.pallas_call(
        kern,
        grid_spec=pltpu.PrefetchScalarGridSpec(
            num_scalar_prefetch=1,
            grid=(N_EXPERTS, tiles),
            in_specs=[any_spec, any_spec, any_spec, any_spec],
            out_specs=any_spec,
            scratch_shapes=[pltpu.VMEM((D_MODEL, D_MODEL), BF16),
                            pltpu.VMEM((D_MODEL, D_MODEL), BF16),
                            pltpu.VMEM((D_MODEL, D_MODEL), BF16),
                            pltpu.VMEM((2, D_MODEL // MOE_CHUNKS, D_MODEL), F32),
                            pltpu.VMEM((2, tm, XH_WIDTH), F32),
                            pltpu.VMEM((2, tm, D_MODEL), F32),
                            pltpu.VMEM((tm, D_MODEL), BF16),
                            pltpu.SemaphoreType.DMA((2,)),
                            pltpu.SemaphoreType.DMA((2,)),
                            pltpu.SemaphoreType.DMA((2,))],
        ),
        out_shape=jax.ShapeDtypeStruct(xh.shape, F32),
        input_output_aliases={1: 0},
        compiler_params=pltpu.CompilerParams(dimension_semantics=("arbitrary", "arbitrary"),
                                             vmem_limit_bytes=VMEM_LIMIT, disable_bounds_checks=True),
        name="moe_ffn",
    )(idx.reshape(-1), xh, w_gate, w_up, w_down)


def _final_kernel(x_ref, g_ref, o_ref):
    o_ref[...] = _rms(x_ref[...], g_ref[...])


def _final_norm(xh, g_final, row0, n):
    tm = min(512, n)
    blk0 = row0 // tm
    assert row0 % tm == 0
    return pl.pallas_call(
        _final_kernel,
        grid=(n // tm,),
        in_specs=[pl.BlockSpec((tm, D_MODEL), lambda i: (blk0 + i, 0)), _const_spec((1, D_MODEL))],
        out_specs=pl.BlockSpec((tm, D_MODEL), lambda i: (i, 0)),
        out_shape=jax.ShapeDtypeStruct((n, D_MODEL), F32),
        compiler_params=_cparams(("parallel",)),
        name="final_norm",
    )(xh, g_final.reshape(1, D_MODEL))


def _forward(xs, mems, p, moe_tm=256):
    t = xs[0].shape[1]
    assert all(x.shape[1] == t for x in xs)
    sizes = [x.shape[0] * t for x in xs]
    starts = [sum(sizes[:j]) for j in range(len(xs))]
    b = sum(x.shape[0] for x in xs)
    wts = (p["w_pool"], p["pool_scale"], p["w_out"], p["g_mq"], p["w_mq"], p["w_mo"], p["g_ffn"],
           p["w_router"])
    q, k, v, u = _in_proj([x.reshape(-1, D_MODEL) for x in xs], p["g_mix"], p["w_in"])
    ona = _na_attention(q, k, v, p["na_bias"], b, t)
    mk, mv = _mem_kv(jnp.concatenate(mems, axis=0), p["g_mkv"], p["w_mk"], p["w_mv"])
    xh, aff_t = _mix(xs, ona, u, mk, mv, wts, t)
    idx = [_route(aff_t[:, row0:row0 + n], row0) for row0, n in zip(starts, sizes)]
    xh = _moe(jnp.concatenate(idx, axis=1), xh, p["w_gate"], p["w_up"], p["w_down"], moe_tm)
    return tuple(_final_norm(xh, p["g_final"], row0, n).reshape(x.shape)
                 for x, row0, n in zip(xs, starts, sizes))


def _prepare(g_mix, w_in, rpb, w_pool, pool_scale, w_out, g_mq, g_mkv, w_mq, w_mk, w_mv, w_mo,
             g_ffn, w_router, w_gate, w_up, w_down, g_final):
    wr = jnp.pad(w_router[0], ((0, 0), (0, LANES - N_EXPERTS)))
    wr_hi = wr.astype(BF16)
    wr_lo = (wr - wr_hi.astype(F32)).astype(BF16)
    row = lambda a: a.reshape(1, -1)
    return dict(
        g_mix=g_mix[0], w_in=w_in[0].astype(BF16), na_bias=_na_bias_table(rpb[0]),
        w_pool=w_pool[0].astype(BF16), pool_scale=row(pool_scale[0]), w_out=w_out[0].astype(BF16),
        g_mq=row(g_mq[0]), g_mkv=g_mkv[0], w_mq=w_mq[0].astype(BF16), w_mk=w_mk[0].astype(BF16),
        w_mv=w_mv[0].astype(BF16), w_mo=w_mo[0].astype(BF16), g_ffn=row(g_ffn[0]),
        w_router=jnp.concatenate([wr_hi, wr_lo], axis=1),
        w_gate=w_gate[0], w_up=w_up[0], w_down=w_down[0], g_final=g_final)


def kernel(x_prompt, x_sample, mem_prompt, mem_sample, g_mix, w_in, rpb, w_pool, pool_scale, w_out,
           g_mq, g_mkv, w_mq, w_mk, w_mv, w_mo, g_ffn, w_router, w_gate, w_up, w_down, g_final):
    p = _prepare(g_mix, w_in, rpb, w_pool, pool_scale, w_out, g_mq, g_mkv, w_mq, w_mk, w_mv, w_mo,
                 g_ffn, w_router, w_gate, w_up, w_down, g_final)
    return _forward((x_prompt, x_sample), (mem_prompt, mem_sample), p)
```

```python
import functools

import jax
import jax.numpy as jnp
from jax import lax
from jax.experimental import pallas as pl
from jax.experimental.pallas import tpu as pltpu

F32 = jnp.float32
BF16 = jnp.bfloat16

D_MODEL = 2048
GRID_W = 64
NA_WIDTH = 1024
NA_HEAD_DIM = 64
NA_HEADS = 16
WIN_ROWS = 8
WIN_COLS = 16
POOL_WINDOWS = (2, 4, 8, 16)
POOL_WIDTH = 1024
POOL_GROUP = 256
MEM_HEADS = 4
MEM_HEAD_DIM = 128
MEM_WIDTH = 512
N_EXPERTS = 16
EC_CAPACITY_FACTOR = 2
EPS = 1e-6
NEG_INF = -1e30
LOG2_E = 1.4426950408889634

LANES = 128
POOL_HALO = 16
XH_WIDTH = 2 * D_MODEL + LANES
VMEM_LIMIT = 56 * 1024 * 1024
MOE_CHUNKS = 8
W_ROWS = 128
W_CHUNKS = D_MODEL // W_ROWS
NA_ROWS = 32
NA_GROUP = 2
MIX_PARTS = 2


def _cparams(sem, vmem=VMEM_LIMIT):
    return pltpu.CompilerParams(dimension_semantics=sem, vmem_limit_bytes=vmem)


def _const_spec(shape):
    nd = len(shape)
    return pl.BlockSpec(shape, lambda *_: (0,) * nd, pipeline_mode=pl.Buffered(1))


def _rms(v, g):
    return v * lax.rsqrt(jnp.mean(v * v, axis=-1, keepdims=True) + EPS) * g


def _pick(step, starts, refs, read):
    val = read(refs[0])
    for start, ref in zip(starts[1:], refs[1:]):
        val = jnp.where(step >= start, read(ref), val)
    return val


def _held_index(step, start, count):
    return jnp.clip(step - start, 0, count - 1)


def _in_proj_kernel(*refs, starts):
    n_src = len(starts)
    x_refs, (g_ref, w_ref), outs = refs[:n_src], refs[n_src:n_src + 2], refs[n_src + 2:]
    x = _pick(pl.program_id(0), starts, x_refs, lambda r: r[...])
    h = _rms(x, g_ref[...]).astype(BF16)
    for j, o_ref in enumerate(outs):
        o_ref[...] = jnp.dot(h, w_ref[:, j * NA_WIDTH:(j + 1) * NA_WIDTH],
                             preferred_element_type=F32).astype(BF16)


def _in_proj(x2ds, g_mix, w_in_bf):
    tm = min(512, *(x.shape[0] for x in x2ds))
    counts = [x.shape[0] // tm for x in x2ds]
    starts = tuple(sum(counts[:j]) for j in range(len(x2ds)))
    n = sum(x.shape[0] for x in x2ds)
    out = jax.ShapeDtypeStruct((n, NA_WIDTH), BF16)
    row = lambda i: (i, 0)
    x_specs = [pl.BlockSpec((tm, D_MODEL), functools.partial(lambda i, s, c: (_held_index(i, s, c), 0), s=s, c=c))
               for s, c in zip(starts, counts)]
    return pl.pallas_call(
        functools.partial(_in_proj_kernel, starts=starts),
        grid=(n // tm,),
        in_specs=x_specs + [_const_spec((1, D_MODEL)), _const_spec((D_MODEL, 4 * NA_WIDTH))],
        out_specs=[pl.BlockSpec((tm, NA_WIDTH), row)] * 4,
        out_shape=[out] * 4,
        compiler_params=_cparams(("parallel",)),
        name="in_proj",
    )(*x2ds, g_mix.reshape(1, D_MODEL), w_in_bf)


def _na_kernel(q_ref, k_ref, v_ref, b_ref, o_ref, *, rows_per_step, n_rows):
    rb = pl.program_id(2)
    lane = lax.broadcasted_iota(jnp.int32, (1, LANES), 1)
    m0 = jnp.where(lane < NA_HEAD_DIM, 1.0, 0.0).astype(BF16)
    m1 = jnp.where(lane >= NA_HEAD_DIM, 1.0, 0.0).astype(BF16)
    lane_f = lax.broadcasted_iota(jnp.int32, (GRID_W, LANES), 1)
    scale = NA_HEAD_DIM ** -0.5 * LOG2_E

    def score_stage(g):
        out = []
        for j in range(NA_GROUP):
            i = g * NA_GROUP + j
            r = rb * rows_per_step + i
            r0 = jnp.clip(r - WIN_ROWS // 2, 0, n_rows - WIN_ROWS)
            qr = q_ref[0, i * GRID_W:(i + 1) * GRID_W, :]
            qq = jnp.concatenate([qr * m0, qr * m1], axis=0)
            kstart = pl.multiple_of(r0 * GRID_W, GRID_W)
            kw = k_ref[0, pl.ds(kstart, WIN_ROWS * GRID_W), :]
            s = lax.dot_general(qq, kw, (((1,), (1,)), ((), ())), preferred_element_type=F32)
            out.append((i, r - r0, kstart, s))
        return out

    def softmax_stage(rows):
        out = []
        for i, d, kstart, s in rows:
            s = s * scale + b_ref[0, d]
            p = jnp.exp2(s - jnp.max(s, axis=-1, keepdims=True))
            out.append((i, kstart, p.astype(BF16), jnp.sum(p, axis=-1, keepdims=True)))
        return out

    def value_stage(rows):
        for i, kstart, p, l in rows:
            vw = v_ref[0, pl.ds(kstart, WIN_ROWS * GRID_W), :]
            o = jnp.dot(p, vw, preferred_element_type=F32) / l
            out = jnp.where(lane_f < NA_HEAD_DIM, o[:GRID_W], o[GRID_W:])
            o_ref[0, i * GRID_W:(i + 1) * GRID_W, :] = out.astype(BF16)

    n_groups = rows_per_step // NA_GROUP
    scored, soft = None, None
    for g in range(n_groups + 2):
        nxt_scored = score_stage(g) if g < n_groups else None
        nxt_soft = softmax_stage(scored) if scored is not None else None
        if soft is not None:
            value_stage(soft)
        scored, soft = nxt_scored, nxt_soft


def _na_bias_table(rpb):
    cols = jnp.arange(GRID_W)
    c0 = jnp.clip(cols - WIN_COLS // 2, 0, GRID_W - WIN_COLS)
    col_mask = (cols[None, :] >= c0[:, None]) & (cols[None, :] < c0[:, None] + WIN_COLS)
    dc_idx = jnp.clip(cols[None, :] - cols[:, None], -(WIN_COLS - 1), WIN_COLS - 1) + WIN_COLS - 1
    onehot = (dc_idx[:, :, None] == jnp.arange(2 * WIN_COLS - 1)).astype(F32)
    t = jnp.einsum('hrc,qkc->hqrk', rpb.astype(F32) * LOG2_E, onehot, precision=lax.Precision.HIGHEST)
    t = jnp.where(col_mask[None, :, None, :], t, NEG_INF)
    per_d = [t[:, :, WIN_ROWS - 1 - d:2 * WIN_ROWS - 1 - d, :]
             .reshape(NA_HEADS // 2, 2 * GRID_W, WIN_ROWS * GRID_W) for d in range(WIN_ROWS)]
    return jnp.stack(per_d, axis=1)


def _na_attention(q, k, v, bias, b, t):
    n_rows = t // GRID_W
    assert n_rows >= WIN_ROWS
    rows_per_step = min(NA_ROWS, n_rows)
    assert rows_per_step % NA_GROUP == 0 and n_rows % rows_per_step == 0
    tq = rows_per_step * GRID_W
    q3, k3, v3 = (a.reshape(b, t, NA_WIDTH) for a in (q, k, v))
    n_pairs = NA_HEADS // 2
    kern = functools.partial(_na_kernel, rows_per_step=rows_per_step, n_rows=n_rows)
    out = pl.pallas_call(
        kern,
        grid=(b, n_pairs, n_rows // rows_per_step),
        in_specs=[pl.BlockSpec((1, tq, LANES), lambda bi, p, r: (bi, r, p)),
                  pl.BlockSpec((1, t, LANES), lambda bi, p, r: (bi, 0, p)),
                  pl.BlockSpec((1, t, LANES), lambda bi, p, r: (bi, 0, p)),
                  pl.BlockSpec((1, WIN_ROWS, 2 * GRID_W, WIN_ROWS * GRID_W),
                               lambda bi, p, r: (p, 0, 0, 0))],
        out_specs=pl.BlockSpec((1, tq, LANES), lambda bi, p, r: (bi, r, p)),
        out_shape=jax.ShapeDtypeStruct((b, t, NA_WIDTH), BF16),
        compiler_params=_cparams(("parallel", "parallel", "parallel")),
        name="na_attention",
    )(q3, k3, v3, bias)
    return out.reshape(b * t, NA_WIDTH)


def _mem_kv_kernel(m_ref, g_ref, wk_ref, wv_ref, k_ref, v_ref):
    h = _rms(m_ref[0], g_ref[...]).astype(BF16)
    k_ref[0] = jnp.dot(h, wk_ref[...], preferred_element_type=F32).astype(BF16)
    v_ref[0] = jnp.dot(h, wv_ref[...], preferred_element_type=F32).astype(BF16)


def _mem_kv(mem, g_mkv, w_mk_bf, w_mv_bf):
    b, m, _ = mem.shape
    out = jax.ShapeDtypeStruct((b, m, MEM_WIDTH), BF16)
    return pl.pallas_call(
        _mem_kv_kernel,
        grid=(b,),
        in_specs=[pl.BlockSpec((1, m, D_MODEL), lambda i: (i, 0, 0)),
                  _const_spec((1, D_MODEL)),
                  _const_spec((D_MODEL, MEM_WIDTH)),
                  _const_spec((D_MODEL, MEM_WIDTH))],
        out_specs=[pl.BlockSpec((1, m, MEM_WIDTH), lambda i: (i, 0, 0))] * 2,
        out_shape=[out] * 2,
        compiler_params=_cparams(("parallel",)),
        name="mem_kv",
    )(mem, g_mkv.reshape(1, D_MODEL), w_mk_bf, w_mv_bf)


def _mix_kernel(*refs, tm, seq, batch_starts):
    x_refs = refs[:len(batch_starts)]
    (ona_ref, up_ref, uc_ref, un_ref, mk_ref, mv_ref, wpool_ref, pscale_ref, wout_ref, gmq_ref,
     wmq_ref, wmo_ref, gffn_ref, wr_ref, xh_ref, afft_ref) = refs[len(batch_starts):]
    t0 = pl.program_id(1) * tm
    pm = tm // MIX_PARTS
    width = pm + 2 * POOL_HALO
    ucat = jnp.concatenate([up_ref[0], uc_ref[0], un_ref[0]], axis=0)
    mk = mk_ref[0]
    mv = mv_ref[0]
    lane = lax.broadcasted_iota(jnp.int32, (pm, LANES), 1)
    parts = [slice(j * pm, (j + 1) * pm) for j in range(MIX_PARTS)]

    def pool(rows):
        p0 = t0 + rows.start
        useg = ucat[rows.start:rows.start + width]
        pos = p0 + lax.broadcasted_iota(jnp.int32, (pm, width), 0)
        src = p0 - POOL_HALO + lax.broadcasted_iota(jnp.int32, (pm, width), 1)
        pcol = p0 + lax.broadcasted_iota(jnp.int32, (pm, 1), 0)
        pools = []
        for g, w in enumerate(POOL_WINDOWS):
            sl = slice(g * POOL_GROUP, (g + 1) * POOL_GROUP)
            lo = jnp.maximum(pos - w // 2, 0)
            hi = jnp.minimum(pos + w // 2, seq)
            band = jnp.where(src >= lo, jnp.where(src < hi, 1.0, 0.0), 0.0).astype(BF16)
            wsum = jnp.dot(band, useg[:, sl], preferred_element_type=F32)
            cnt = (jnp.minimum(pcol + w // 2, seq) - jnp.maximum(pcol - w // 2, 0)).astype(F32)
            dlt = (wsum / cnt - useg[POOL_HALO:POOL_HALO + pm, sl].astype(F32)).astype(BF16)
            pools.append(jnp.dot(dlt, wpool_ref[g], preferred_element_type=F32))
        return (jnp.concatenate(pools, axis=1) * pscale_ref[...]).astype(BF16)

    def mix(rows, o_pool):
        x = _pick(pl.program_id(0), batch_starts, x_refs, lambda r: r[0, rows, :])
        return (x
                + jnp.dot(ona_ref[0, rows, :], wout_ref[:NA_WIDTH, :], preferred_element_type=F32)
                + jnp.dot(o_pool, wout_ref[NA_WIDTH:, :], preferred_element_type=F32))

    def mem_query(x1):
        hq = _rms(x1, gmq_ref[...]).astype(BF16)
        return jnp.dot(hq, wmq_ref[...], preferred_element_type=F32).astype(BF16)

    def mem_attend(q):
        heads = []
        for h in range(MEM_HEADS):
            sl = slice(h * MEM_HEAD_DIM, (h + 1) * MEM_HEAD_DIM)
            s = lax.dot_general(q[:, sl], mk[:, sl], (((1,), (1,)), ((), ())),
                                preferred_element_type=F32) * (MEM_HEAD_DIM ** -0.5)
            p = jnp.exp(s - jnp.max(s, axis=-1, keepdims=True))
            l = jnp.sum(p, axis=-1, keepdims=True)
            heads.append(jnp.dot(p.astype(BF16), mv[:, sl], preferred_element_type=F32) / l)
        return jnp.concatenate(heads, axis=1).astype(BF16)

    def mem_out(x1, o_mem):
        return x1 + jnp.dot(o_mem, wmo_ref[...], preferred_element_type=F32)

    def route(rows, x2):
        h3 = _rms(x2, gffn_ref[...])
        h_hi = h3.astype(BF16)
        h_lo = (h3 - h_hi.astype(F32)).astype(BF16)
        both = jnp.dot(h_hi, wr_ref[...], preferred_element_type=F32)
        logits = (both[:, :LANES] + both[:, LANES:]
                  + jnp.dot(h_lo, wr_ref[:, :LANES], preferred_element_type=F32))
        logits = jnp.where(lane < N_EXPERTS, logits, NEG_INF)
        ex = jnp.exp(logits - jnp.max(logits, axis=-1, keepdims=True))
        aff = ex / jnp.sum(ex, axis=-1, keepdims=True)
        xh_ref[rows, 0:D_MODEL] = x2
        xh_ref[rows, D_MODEL:2 * D_MODEL] = h3
        xh_ref[rows, 2 * D_MODEL:] = aff
        afft_ref[:, rows] = aff.T[:N_EXPERTS, :]

    o_pool = [pool(r) for r in parts]
    x1 = [mix(r, o) for r, o in zip(parts, o_pool)]
    q = [mem_query(v) for v in x1]
    o_mem = [mem_attend(v) for v in q]
    x2 = [mem_out(a, o) for a, o in zip(x1, o_mem)]
    for r, v in zip(parts, x2):
        route(r, v)


def _mix(xs, ona, u, mk, mv, wts, t):
    tm = min(256, t)
    nt = t // tm
    hb = tm // POOL_HALO
    last_hb = t // POOL_HALO - 1
    batches = [x.shape[0] for x in xs]
    batch_starts = tuple(sum(batches[:j]) for j in range(len(xs)))
    b = sum(batches)
    n = b * t
    ona3 = ona.reshape(b, t, NA_WIDTH)
    u3 = u.reshape(b, t, POOL_WIDTH)

    def x_spec(start, count):
        def index(bi, i):
            tile = jnp.where(bi < start, 0, jnp.where(bi >= start + count, nt - 1, i))
            return (_held_index(bi, start, count), tile, 0)
        return pl.BlockSpec((1, tm, D_MODEL), index)

    kern = functools.partial(_mix_kernel, tm=tm, seq=t, batch_starts=batch_starts)
    return pl.pallas_call(
        kern,
        grid=(b, nt),
        in_specs=[x_spec(s, c) for s, c in zip(batch_starts, batches)] + [
            pl.BlockSpec((1, tm, NA_WIDTH), lambda bi, i: (bi, i, 0)),
            pl.BlockSpec((1, POOL_HALO, POOL_WIDTH), lambda bi, i: (bi, jnp.maximum(i * hb - 1, 0), 0)),
            pl.BlockSpec((1, tm, POOL_WIDTH), lambda bi, i: (bi, i, 0)),
            pl.BlockSpec((1, POOL_HALO, POOL_WIDTH),
                         lambda bi, i: (bi, jnp.minimum((i + 1) * hb, last_hb), 0)),
            pl.BlockSpec((1, mk.shape[1], MEM_WIDTH), lambda bi, i: (bi, 0, 0)),
            pl.BlockSpec((1, mk.shape[1], MEM_WIDTH), lambda bi, i: (bi, 0, 0)),
            _const_spec((len(POOL_WINDOWS), POOL_GROUP, POOL_GROUP)),
            _const_spec((1, POOL_WIDTH)),
            _const_spec((D_MODEL, D_MODEL)),
            _const_spec((1, D_MODEL)),
            _const_spec((D_MODEL, MEM_WIDTH)),
            _const_spec((MEM_WIDTH, D_MODEL)),
            _const_spec((1, D_MODEL)),
            _const_spec((D_MODEL, 2 * LANES)),
        ],
        out_specs=[pl.BlockSpec((tm, XH_WIDTH), lambda bi, i: (bi * nt + i, 0)),
                   pl.BlockSpec((N_EXPERTS, tm), lambda bi, i: (0, bi * nt + i))],
        out_shape=[jax.ShapeDtypeStruct((n, XH_WIDTH), F32),
                   jax.ShapeDtypeStruct((N_EXPERTS, n), F32)],
        compiler_params=_cparams(("parallel", "parallel")),
        name="mix_mem_router",
    )(*xs, ona3, u3, u3, u3, mk, mv, *wts)


def _total(v):
    return jnp.sum(jnp.sum(v, axis=1, keepdims=True), axis=0, keepdims=True)


def _bits_as_float(bits):
    return pltpu.bitcast(bits, F32)[0:1, 0:1]


def _threshold_kernel(a_ref, thr_ref, *, cap):
    capf = jnp.float32(cap)

    def search(b, los):
        bit = jnp.left_shift(jnp.int32(1), 30 - b)
        out = []
        for e in range(N_EXPERTS):
            cand = los[e] | bit
            cnt = _total(jnp.where(a_ref[e] >= _bits_as_float(cand), 1.0, 0.0))
            out.append(jnp.where(cnt >= capf, cand, los[e]))
        return tuple(out)

    los = lax.fori_loop(0, 31, search, tuple(jnp.zeros((8, LANES), jnp.int32) for _ in range(N_EXPERTS)))
    for e in range(N_EXPERTS):
        thr_ref[e] = los[e]


def _route_kernel(a_ref, thr_ref, idx_ref, *, nch, cap):
    a = a_ref[0]
    capf = jnp.float32(cap)
    total, as_float = _total, _bits_as_float
    thr_bits = thr_ref[0]
    thr = as_float(thr_bits)
    gt = a >= as_float(thr_bits + 1)
    eq = jnp.logical_and(a >= thr, jnp.logical_not(gt))
    need = capf - total(jnp.where(gt, 1.0, 0.0))

    kr = lax.broadcasted_iota(jnp.int32, (LANES, LANES), 0)
    kc = lax.broadcasted_iota(jnp.int32, (LANES, LANES), 1)
    tri_lane = jnp.where(kr <= kc, 1.0, 0.0).astype(BF16)
    cr = lax.broadcasted_iota(jnp.int32, (nch, nch), 0)
    cc = lax.broadcasted_iota(jnp.int32, (nch, nch), 1)
    tri_excl = jnp.where(cc < cr, 1.0, 0.0).astype(BF16)
    tri_incl = jnp.where(cc <= cr, 1.0, 0.0).astype(BF16)

    def prefix(maskf, chunk_tri):
        within = jnp.dot(maskf.astype(BF16), tri_lane, preferred_element_type=F32)
        tot = jnp.broadcast_to(within[:, LANES - 1:LANES], (nch, LANES))
        before = jnp.dot(chunk_tri, tot.astype(BF16), preferred_element_type=F32)
        return within, before

    eqf = jnp.where(eq, 1.0, 0.0)
    eq_within, eq_before = prefix(eqf, tri_excl)
    eq_rank = eq_before + eq_within - eqf
    self_ = jnp.where(gt, 1.0, jnp.where(eq, jnp.where(eq_rank < need, 1.0, 0.0), 0.0))

    within, cs_incl = prefix(self_, tri_incl)
    cs_col = cs_incl[:, 0:1]
    j = lax.broadcasted_iota(jnp.int32, (1, cap), 1).astype(F32)
    done = cs_col <= j
    chunk = jnp.sum(jnp.where(done, 1.0, 0.0), axis=0, keepdims=True)
    cs_prev = jnp.max(jnp.where(done, cs_col, 0.0), axis=0, keepdims=True)
    crow = lax.broadcasted_iota(jnp.int32, (nch, cap), 0).astype(F32)
    onehot = jnp.where(crow == chunk, 1.0, 0.0).astype(BF16)
    wsel = jnp.dot(within.T.astype(BF16), onehot, preferred_element_type=F32)
    rank = j - cs_prev
    inner = jnp.sum(jnp.where(wsel <= rank, 1.0, 0.0), axis=0, keepdims=True)
    idx_ref[0] = (chunk * LANES + inner).astype(jnp.int32)


def _route(aff_t, row0):
    n = aff_t.shape[1]
    cap = EC_CAPACITY_FACTOR * n // N_EXPERTS
    nch = n // LANES
    aff3 = aff_t.reshape(N_EXPERTS, nch, LANES)
    thr = pl.pallas_call(
        functools.partial(_threshold_kernel, cap=cap),
        grid=(1,),
        in_specs=[pl.BlockSpec((N_EXPERTS, nch, LANES), lambda i: (0, 0, 0))],
        out_specs=pl.BlockSpec((N_EXPERTS, 8, LANES), lambda i: (0, 0, 0)),
        out_shape=jax.ShapeDtypeStruct((N_EXPERTS, 8, LANES), jnp.int32),
        compiler_params=_cparams(("arbitrary",)),
        name="route_threshold",
    )(aff3)
    kern = functools.partial(_route_kernel, nch=nch, cap=cap)
    idx = pl.pallas_call(
        kern,
        grid=(N_EXPERTS,),
        in_specs=[pl.BlockSpec((1, nch, LANES), lambda e: (e, 0, 0)),
                  pl.BlockSpec((1, 8, LANES), lambda e: (e, 0, 0))],
        out_specs=pl.BlockSpec((1, 1, cap), lambda e: (e, 0, 0)),
        out_shape=jax.ShapeDtypeStruct((N_EXPERTS, 1, cap), jnp.int32),
        compiler_params=_cparams(("parallel",)),
        name="route",
    )(aff3, thr)
    return idx.reshape(N_EXPERTS, cap) + row0


def _moe_kernel(idx_ref, xh_in_ref, wg_hbm, wu_hbm, wd_hbm, xh_ref, wbf_hbm,
                wg_ref, wu_ref, wd_ref, wstage, bstage, xbuf, obuf, hid_ref, gsem, ssem, wsem, bsem, lsem,
                *, tm, tiles):
    del xh_in_ref
    e = pl.program_id(0)
    i = pl.program_id(1)
    s = e * tiles + i
    slot = lax.rem(s, 2)
    other = 1 - slot
    base = s * tm
    mats = ((wg_hbm, wg_ref), (wu_hbm, wu_ref), (wd_hbm, wd_ref))

    def gather_copy(tok, r, sl):
        return pltpu.make_async_copy(xh_ref.at[pl.ds(tok, 1), :], xbuf.at[sl, pl.ds(r, 1), :], gsem.at[sl])

    def scatter_copy(tok, r, sl):
        return pltpu.make_async_copy(obuf.at[sl, pl.ds(r, 1), :],
                                     xh_ref.at[pl.ds(tok, 1), pl.ds(0, D_MODEL)], ssem.at[sl])

    def start_rows(copy, first, sl):
        def body(r, c):
            copy(idx_ref[first + r], r, sl).start()
            return c
        lax.fori_loop(0, tm, body, 0, unroll=8)

    def wait_rows(copy, sl):
        for _ in range(tm):
            copy(0, 0, sl).wait()

    def load_weights():
        jobs = [(src, dst, c) for src, dst in mats for c in range(W_CHUNKS)]

        def chunk_copy(k):
            src, _, c = jobs[k]
            return pltpu.make_async_copy(src.at[e, pl.ds(c * W_ROWS, W_ROWS), :], wstage.at[k % 2], wsem.at[k % 2])

        chunk_copy(0).start()
        for k, (_, dst, c) in enumerate(jobs):
            if k + 1 < len(jobs):
                chunk_copy(k + 1).start()
            chunk_copy(k).wait()
            dst[c * W_ROWS:(c + 1) * W_ROWS, :] = wstage[k % 2].astype(BF16)

    active = 1 << (min(tiles, W_CHUNKS).bit_length() - 1)
    per_step = W_CHUNKS // active
    nxt = lax.rem(e + 1, N_EXPERTS)
    conv_jobs = [(m, j) for m in range(len(mats)) for j in range(per_step)]

    def conv_row(tile, j):
        return pl.multiple_of((lax.rem(tile, active) * per_step + j) * W_ROWS, W_ROWS)

    def conv_in(m, j, tile):
        return pltpu.make_async_copy(mats[m][0].at[nxt, pl.ds(conv_row(tile, j), W_ROWS), :],
                                     wstage.at[m * per_step + j], wsem.at[0])

    def conv_out(m, j):
        return pltpu.make_async_copy(bstage.at[m * per_step + j],
                                     wbf_hbm.at[nxt, pl.ds(m * D_MODEL + conv_row(i, j), W_ROWS), :], bsem.at[0])

    def weight_load(m, expert):
        return pltpu.make_async_copy(wbf_hbm.at[expert, pl.ds(m * D_MODEL, D_MODEL), :], mats[m][1], lsem.at[0])

    @pl.when(i >= 1)
    def _():
        wait_rows(scatter_copy, slot)
        for m, j in conv_jobs:
            conv_out(m, j).wait()

    @pl.when(i == 0)
    def _():
        start_rows(gather_copy, base, slot)

        @pl.when(e == 0)
        def _():
            load_weights()

        @pl.when(e >= 1)
        def _():
            for m in range(len(mats)):
                weight_load(m, e).wait()

        for m, j in conv_jobs:
            conv_in(m, j, i).start()

    wait_rows(gather_copy, slot)
    for m, j in conv_jobs:
        conv_in(m, j, i).wait()
    for m, j in conv_jobs:
        bstage[m * per_step + j] = wstage[m * per_step + j].astype(BF16)
        conv_out(m, j).start()

    @pl.when(i == 0)
    def _():
        obuf[other] = xbuf[slot, :, 0:D_MODEL]

    wb_first = jnp.where(i == 0, base, base - tm)
    pf_first = jnp.where(i + 1 < tiles, base + tm, base)
    rows_per_chunk = tm // MOE_CHUNKS
    fw = D_MODEL // MOE_CHUNKS
    xs = xbuf[slot, :, D_MODEL:2 * D_MODEL].astype(BF16)
    for c in range(MOE_CHUNKS):
        a = jnp.dot(xs, wg_ref[:, c * fw:(c + 1) * fw], preferred_element_type=F32)
        up = jnp.dot(xs, wu_ref[:, c * fw:(c + 1) * fw], preferred_element_type=F32)
        hid_ref[:, c * fw:(c + 1) * fw] = (a * jax.nn.sigmoid(a) * up).astype(BF16)
        for r in range(c * rows_per_chunk, (c + 1) * rows_per_chunk):
            scatter_copy(idx_ref[wb_first + r], r, other).start()
            gather_copy(idx_ref[pf_first + r], r, other).start()
    y = jnp.dot(hid_ref[...], wd_ref[...], preferred_element_type=F32)
    lane = lax.broadcasted_iota(jnp.int32, (tm, LANES), 1)
    gate = jnp.sum(jnp.where(lane == e, xbuf[slot, :, 2 * D_MODEL:], 0.0), axis=1, keepdims=True)
    obuf[slot] = xbuf[slot, :, 0:D_MODEL] + y * gate

    @pl.when(i + 1 < tiles)
    def _():
        for m, j in conv_jobs:
            conv_in(m, j, i + 1).start()

    @pl.when(i == tiles - 1)
    def _():
        for m, j in conv_jobs:
            conv_out(m, j).wait()

        @pl.when(e + 1 < N_EXPERTS)
        def _():
            for m in range(len(mats)):
                weight_load(m, e + 1).start()

        wait_rows(gather_copy, other)
        wait_rows(scatter_copy, other)
        start_rows(scatter_copy, base, slot)
        wait_rows(scatter_copy, slot)


def _moe(idx, xh, w_gate, w_up, w_down, tm):
    cap = idx.shape[1]
    tiles = cap // tm
    assert cap % tm == 0 and tiles >= 2 and tm % MOE_CHUNKS == 0 and MOE_CHUNKS >= 2
    per_step = W_CHUNKS // (1 << (min(tiles, W_CHUNKS).bit_length() - 1))
    kern = functools.partial(_moe_kernel, tm=tm, tiles=tiles)
    any_spec = pl.BlockSpec(memory_space=pl.ANY)
    xh, _ = pl.pallas_call(
        kern,
        grid_spec=pltpu.PrefetchScalarGridSpec(
            num_scalar_prefetch=1,
            grid=(N_EXPERTS, tiles),
            in_specs=[any_spec, any_spec, any_spec, any_spec],
            out_specs=[any_spec, any_spec],
            scratch_shapes=[pltpu.VMEM((D_MODEL, D_MODEL), BF16),
                            pltpu.VMEM((D_MODEL, D_MODEL), BF16),
                            pltpu.VMEM((D_MODEL, D_MODEL), BF16),
                            pltpu.VMEM((max(2, 3 * per_step), W_ROWS, D_MODEL), F32),
                            pltpu.VMEM((3 * per_step, W_ROWS, D_MODEL), BF16),
                            pltpu.VMEM((2, tm, XH_WIDTH), F32),
                            pltpu.VMEM((2, tm, D_MODEL), F32),
                            pltpu.VMEM((tm, D_MODEL), BF16),
                            pltpu.SemaphoreType.DMA((2,)),
                            pltpu.SemaphoreType.DMA((2,)),
                            pltpu.SemaphoreType.DMA((2,)),
                            pltpu.SemaphoreType.DMA((1,)),
                            pltpu.SemaphoreType.DMA((1,))],
        ),
        out_shape=[jax.ShapeDtypeStruct(xh.shape, F32),
                   jax.ShapeDtypeStruct((N_EXPERTS, 3 * D_MODEL, D_MODEL), BF16)],
        input_output_aliases={1: 0},
        compiler_params=pltpu.CompilerParams(dimension_semantics=("arbitrary", "arbitrary"),
                                             vmem_limit_bytes=VMEM_LIMIT, disable_bounds_checks=True),
        name="moe_ffn",
    )(idx.reshape(-1), xh, w_gate, w_up, w_down)
    return xh


def _final_kernel(x_ref, g_ref, o_ref):
    o_ref[...] = _rms(x_ref[...], g_ref[...])


def _final_norm(xh, g_final, row0, n):
    tm = min(512, n)
    blk0 = row0 // tm
    assert row0 % tm == 0
    return pl.pallas_call(
        _final_kernel,
        grid=(n // tm,),
        in_specs=[pl.BlockSpec((tm, D_MODEL), lambda i: (blk0 + i, 0)), _const_spec((1, D_MODEL))],
        out_specs=pl.BlockSpec((tm, D_MODEL), lambda i: (i, 0)),
        out_shape=jax.ShapeDtypeStruct((n, D_MODEL), F32),
        compiler_params=_cparams(("parallel",)),
        name="final_norm",
    )(xh, g_final.reshape(1, D_MODEL))


def _forward(xs, mems, p, moe_tm=256):
    t = xs[0].shape[1]
    assert all(x.shape[1] == t for x in xs)
    sizes = [x.shape[0] * t for x in xs]
    starts = [sum(sizes[:j]) for j in range(len(xs))]
    b = sum(x.shape[0] for x in xs)
    wts = (p["w_pool"], p["pool_scale"], p["w_out"], p["g_mq"], p["w_mq"], p["w_mo"], p["g_ffn"],
           p["w_router"])
    q, k, v, u = _in_proj([x.reshape(-1, D_MODEL) for x in xs], p["g_mix"], p["w_in"])
    ona = _na_attention(q, k, v, p["na_bias"], b, t)
    mk, mv = _mem_kv(jnp.concatenate(mems, axis=0), p["g_mkv"], p["w_mk"], p["w_mv"])
    xh, aff_t = _mix(xs, ona, u, mk, mv, wts, t)
    idx = [_route(aff_t[:, row0:row0 + n], row0) for row0, n in zip(starts, sizes)]
    xh = _moe(jnp.concatenate(idx, axis=1), xh, p["w_gate"], p["w_up"], p["w_down"], moe_tm)
    return tuple(_final_norm(xh, p["g_final"], row0, n).reshape(x.shape)
                 for x, row0, n in zip(xs, starts, sizes))


def _prepare(g_mix, w_in, rpb, w_pool, pool_scale, w_out, g_mq, g_mkv, w_mq, w_mk, w_mv, w_mo,
             g_ffn, w_router, w_gate, w_up, w_down, g_final):
    wr = jnp.pad(w_router[0], ((0, 0), (0, LANES - N_EXPERTS)))
    wr_hi = wr.astype(BF16)
    wr_lo = (wr - wr_hi.astype(F32)).astype(BF16)
    row = lambda a: a.reshape(1, -1)
    return dict(
        g_mix=g_mix[0], w_in=w_in[0].astype(BF16), na_bias=_na_bias_table(rpb[0]),
        w_pool=w_pool[0].astype(BF16), pool_scale=row(pool_scale[0]), w_out=w_out[0].astype(BF16),
        g_mq=row(g_mq[0]), g_mkv=g_mkv[0], w_mq=w_mq[0].astype(BF16), w_mk=w_mk[0].astype(BF16),
        w_mv=w_mv[0].astype(BF16), w_mo=w_mo[0].astype(BF16), g_ffn=row(g_ffn[0]),
        w_router=jnp.concatenate([wr_hi, wr_lo], axis=1),
        w_gate=w_gate[0], w_up=w_up[0], w_down=w_down[0], g_final=g_final)


def kernel(x_prompt, x_sample, mem_prompt, mem_sample, g_mix, w_in, rpb, w_pool, pool_scale, w_out,
           g_mq, g_mkv, w_mq, w_mk, w_mv, w_mo, g_ffn, w_router, w_gate, w_up, w_down, g_final):
    p = _prepare(g_mix, w_in, rpb, w_pool, pool_scale, w_out, g_mq, g_mkv, w_mq, w_mk, w_mv, w_mo,
                 g_ffn, w_router, w_gate, w_up, w_down, g_final)
    return _forward((x_prompt, x_sample), (mem_prompt, mem_sample), p)
```

```python
import functools

import jax
import jax.numpy as jnp
from jax import lax
from jax.experimental import pallas as pl
from jax.experimental.pallas import tpu as pltpu

F32 = jnp.float32
BF16 = jnp.bfloat16

D_MODEL = 2048
GRID_W = 64
NA_WIDTH = 1024
NA_HEAD_DIM = 64
NA_HEADS = 16
WIN_ROWS = 8
WIN_COLS = 16
POOL_WINDOWS = (2, 4, 8, 16)
POOL_WIDTH = 1024
POOL_GROUP = 256
MEM_HEADS = 4
MEM_HEAD_DIM = 128
MEM_WIDTH = 512
N_EXPERTS = 16
EC_CAPACITY_FACTOR = 2
EPS = 1e-6
NEG_INF = -1e30
LOG2_E = 1.4426950408889634

LANES = 128
POOL_HALO = 16
XH_WIDTH = 2 * D_MODEL + LANES
VMEM_LIMIT = 56 * 1024 * 1024
MOE_CHUNKS = 8
W_ROWS = 128
W_CHUNKS = D_MODEL // W_ROWS
NA_ROWS = 32
NA_GROUP = 2
MIX_PARTS = 2


def _cparams(sem, vmem=VMEM_LIMIT):
    return pltpu.CompilerParams(dimension_semantics=sem, vmem_limit_bytes=vmem)


def _const_spec(shape):
    nd = len(shape)
    return pl.BlockSpec(shape, lambda *_: (0,) * nd, pipeline_mode=pl.Buffered(1))


def _rms(v, g):
    return v * lax.rsqrt(jnp.mean(v * v, axis=-1, keepdims=True) + EPS) * g


def _pick(step, starts, refs, read):
    val = read(refs[0])
    for start, ref in zip(starts[1:], refs[1:]):
        val = jnp.where(step >= start, read(ref), val)
    return val


def _held_index(step, start, count):
    return jnp.clip(step - start, 0, count - 1)


def _in_proj_kernel(*refs, starts):
    n_src = len(starts)
    x_refs, (g_ref, w_ref), outs = refs[:n_src], refs[n_src:n_src + 2], refs[n_src + 2:]
    x = _pick(pl.program_id(0), starts, x_refs, lambda r: r[...])
    h = _rms(x, g_ref[...]).astype(BF16)
    for j, o_ref in enumerate(outs):
        o_ref[...] = jnp.dot(h, w_ref[:, j * NA_WIDTH:(j + 1) * NA_WIDTH],
                             preferred_element_type=F32).astype(BF16)


def _in_proj(x2ds, g_mix, w_in_bf):
    tm = min(512, *(x.shape[0] for x in x2ds))
    counts = [x.shape[0] // tm for x in x2ds]
    starts = tuple(sum(counts[:j]) for j in range(len(x2ds)))
    n = sum(x.shape[0] for x in x2ds)
    out = jax.ShapeDtypeStruct((n, NA_WIDTH), BF16)
    row = lambda i: (i, 0)
    x_specs = [pl.BlockSpec((tm, D_MODEL), functools.partial(lambda i, s, c: (_held_index(i, s, c), 0), s=s, c=c))
               for s, c in zip(starts, counts)]
    return pl.pallas_call(
        functools.partial(_in_proj_kernel, starts=starts),
        grid=(n // tm,),
        in_specs=x_specs + [_const_spec((1, D_MODEL)), _const_spec((D_MODEL, 4 * NA_WIDTH))],
        out_specs=[pl.BlockSpec((tm, NA_WIDTH), row)] * 4,
        out_shape=[out] * 4,
        compiler_params=_cparams(("parallel",)),
        name="in_proj",
    )(*x2ds, g_mix.reshape(1, D_MODEL), w_in_bf)


def _na_kernel(q_ref, k_ref, v_ref, b_ref, o_ref, *, rows_per_step, n_rows):
    rb = pl.program_id(2)
    lane = lax.broadcasted_iota(jnp.int32, (1, LANES), 1)
    m0 = jnp.where(lane < NA_HEAD_DIM, 1.0, 0.0).astype(BF16)
    m1 = jnp.where(lane >= NA_HEAD_DIM, 1.0, 0.0).astype(BF16)
    lane_f = lax.broadcasted_iota(jnp.int32, (GRID_W, LANES), 1)
    scale = NA_HEAD_DIM ** -0.5 * LOG2_E

    def score_stage(g):
        out = []
        for j in range(NA_GROUP):
            i = g * NA_GROUP + j
            r = rb * rows_per_step + i
            r0 = jnp.clip(r - WIN_ROWS // 2, 0, n_rows - WIN_ROWS)
            qr = q_ref[0, i * GRID_W:(i + 1) * GRID_W, :]
            qq = jnp.concatenate([qr * m0, qr * m1], axis=0)
            kstart = pl.multiple_of(r0 * GRID_W, GRID_W)
            kw = k_ref[0, pl.ds(kstart, WIN_ROWS * GRID_W), :]
            s = lax.dot_general(qq, kw, (((1,), (1,)), ((), ())), preferred_element_type=F32)
            out.append((i, r - r0, kstart, s))
        return out

    def softmax_stage(rows):
        out = []
        for i, d, kstart, s in rows:
            s = s * scale + b_ref[0, d]
            p = jnp.exp2(s - jnp.max(s, axis=-1, keepdims=True))
            out.append((i, kstart, p.astype(BF16), jnp.sum(p, axis=-1, keepdims=True)))
        return out

    def value_stage(rows):
        for i, kstart, p, l in rows:
            vw = v_ref[0, pl.ds(kstart, WIN_ROWS * GRID_W), :]
            o = jnp.dot(p, vw, preferred_element_type=F32) / l
            out = jnp.where(lane_f < NA_HEAD_DIM, o[:GRID_W], o[GRID_W:])
            o_ref[0, i * GRID_W:(i + 1) * GRID_W, :] = out.astype(BF16)

    n_groups = rows_per_step // NA_GROUP
    scored, soft = None, None
    for g in range(n_groups + 2):
        nxt_scored = score_stage(g) if g < n_groups else None
        nxt_soft = softmax_stage(scored) if scored is not None else None
        if soft is not None:
            value_stage(soft)
        scored, soft = nxt_scored, nxt_soft


def _na_bias_table(rpb):
    cols = jnp.arange(GRID_W)
    c0 = jnp.clip(cols - WIN_COLS // 2, 0, GRID_W - WIN_COLS)
    col_mask = (cols[None, :] >= c0[:, None]) & (cols[None, :] < c0[:, None] + WIN_COLS)
    dc_idx = jnp.clip(cols[None, :] - cols[:, None], -(WIN_COLS - 1), WIN_COLS - 1) + WIN_COLS - 1
    onehot = (dc_idx[:, :, None] == jnp.arange(2 * WIN_COLS - 1)).astype(F32)
    t = jnp.einsum('hrc,qkc->hqrk', rpb.astype(F32) * LOG2_E, onehot, precision=lax.Precision.HIGHEST)
    t = jnp.where(col_mask[None, :, None, :], t, NEG_INF)
    per_d = [t[:, :, WIN_ROWS - 1 - d:2 * WIN_ROWS - 1 - d, :]
             .reshape(NA_HEADS // 2, 2 * GRID_W, WIN_ROWS * GRID_W) for d in range(WIN_ROWS)]
    return jnp.stack(per_d, axis=1)


def _na_attention(q, k, v, bias, b, t):
    n_rows = t // GRID_W
    assert n_rows >= WIN_ROWS
    rows_per_step = min(NA_ROWS, n_rows)
    assert rows_per_step % NA_GROUP == 0 and n_rows % rows_per_step == 0
    tq = rows_per_step * GRID_W
    q3, k3, v3 = (a.reshape(b, t, NA_WIDTH) for a in (q, k, v))
    n_pairs = NA_HEADS // 2
    kern = functools.partial(_na_kernel, rows_per_step=rows_per_step, n_rows=n_rows)
    out = pl.pallas_call(
        kern,
        grid=(b, n_pairs, n_rows // rows_per_step),
        in_specs=[pl.BlockSpec((1, tq, LANES), lambda bi, p, r: (bi, r, p)),
                  pl.BlockSpec((1, t, LANES), lambda bi, p, r: (bi, 0, p)),
                  pl.BlockSpec((1, t, LANES), lambda bi, p, r: (bi, 0, p)),
                  pl.BlockSpec((1, WIN_ROWS, 2 * GRID_W, WIN_ROWS * GRID_W),
                               lambda bi, p, r: (p, 0, 0, 0))],
        out_specs=pl.BlockSpec((1, tq, LANES), lambda bi, p, r: (bi, r, p)),
        out_shape=jax.ShapeDtypeStruct((b, t, NA_WIDTH), BF16),
        compiler_params=_cparams(("parallel", "parallel", "parallel")),
        name="na_attention",
    )(q3, k3, v3, bias)
    return out.reshape(b * t, NA_WIDTH)


def _mem_kv_kernel(m_ref, g_ref, wk_ref, wv_ref, k_ref, v_ref):
    h = _rms(m_ref[0], g_ref[...]).astype(BF16)
    k_ref[0] = jnp.dot(h, wk_ref[...], preferred_element_type=F32).astype(BF16)
    v_ref[0] = jnp.dot(h, wv_ref[...], preferred_element_type=F32).astype(BF16)


def _mem_kv(mem, g_mkv, w_mk_bf, w_mv_bf):
    b, m, _ = mem.shape
    out = jax.ShapeDtypeStruct((b, m, MEM_WIDTH), BF16)
    return pl.pallas_call(
        _mem_kv_kernel,
        grid=(b,),
        in_specs=[pl.BlockSpec((1, m, D_MODEL), lambda i: (i, 0, 0)),
                  _const_spec((1, D_MODEL)),
                  _const_spec((D_MODEL, MEM_WIDTH)),
                  _const_spec((D_MODEL, MEM_WIDTH))],
        out_specs=[pl.BlockSpec((1, m, MEM_WIDTH), lambda i: (i, 0, 0))] * 2,
        out_shape=[out] * 2,
        compiler_params=_cparams(("parallel",)),
        name="mem_kv",
    )(mem, g_mkv.reshape(1, D_MODEL), w_mk_bf, w_mv_bf)


def _mix_kernel(*refs, tm, seq, batch_starts):
    x_refs = refs[:len(batch_starts)]
    (ona_ref, up_ref, uc_ref, un_ref, mk_ref, mv_ref, wpool_ref, pscale_ref, wout_ref, gmq_ref,
     wmq_ref, wmo_ref, gffn_ref, wr_ref, xh_ref, afft_ref) = refs[len(batch_starts):]
    t0 = pl.program_id(1) * tm
    pm = tm // MIX_PARTS
    width = pm + 2 * POOL_HALO
    ucat = jnp.concatenate([up_ref[0], uc_ref[0], un_ref[0]], axis=0)
    mk = mk_ref[0]
    mv = mv_ref[0]
    lane = lax.broadcasted_iota(jnp.int32, (pm, LANES), 1)
    parts = [slice(j * pm, (j + 1) * pm) for j in range(MIX_PARTS)]

    def pool(rows):
        p0 = t0 + rows.start
        useg = ucat[rows.start:rows.start + width]
        pos = p0 + lax.broadcasted_iota(jnp.int32, (pm, width), 0)
        src = p0 - POOL_HALO + lax.broadcasted_iota(jnp.int32, (pm, width), 1)
        pcol = p0 + lax.broadcasted_iota(jnp.int32, (pm, 1), 0)
        pools = []
        for g, w in enumerate(POOL_WINDOWS):
            sl = slice(g * POOL_GROUP, (g + 1) * POOL_GROUP)
            lo = jnp.maximum(pos - w // 2, 0)
            hi = jnp.minimum(pos + w // 2, seq)
            band = jnp.where(src >= lo, jnp.where(src < hi, 1.0, 0.0), 0.0).astype(BF16)
            wsum = jnp.dot(band, useg[:, sl], preferred_element_type=F32)
            cnt = (jnp.minimum(pcol + w // 2, seq) - jnp.maximum(pcol - w // 2, 0)).astype(F32)
            dlt = (wsum / cnt - useg[POOL_HALO:POOL_HALO + pm, sl].astype(F32)).astype(BF16)
            pools.append(jnp.dot(dlt, wpool_ref[g], preferred_element_type=F32))
        return (jnp.concatenate(pools, axis=1) * pscale_ref[...]).astype(BF16)

    def mix(rows, o_pool):
        x = _pick(pl.program_id(0), batch_starts, x_refs, lambda r: r[0, rows, :])
        return (x
                + jnp.dot(ona_ref[0, rows, :], wout_ref[:NA_WIDTH, :], preferred_element_type=F32)
                + jnp.dot(o_pool, wout_ref[NA_WIDTH:, :], preferred_element_type=F32))

    def mem_query(x1):
        hq = _rms(x1, gmq_ref[...]).astype(BF16)
        return jnp.dot(hq, wmq_ref[...], preferred_element_type=F32).astype(BF16)

    def mem_attend(q):
        heads = []
        for h in range(MEM_HEADS):
            sl = slice(h * MEM_HEAD_DIM, (h + 1) * MEM_HEAD_DIM)
            s = lax.dot_general(q[:, sl], mk[:, sl], (((1,), (1,)), ((), ())),
                                preferred_element_type=F32) * (MEM_HEAD_DIM ** -0.5)
            p = jnp.exp(s - jnp.max(s, axis=-1, keepdims=True))
            l = jnp.sum(p, axis=-1, keepdims=True)
            heads.append(jnp.dot(p.astype(BF16), mv[:, sl], preferred_element_type=F32) / l)
        return jnp.concatenate(heads, axis=1).astype(BF16)

    def mem_out(x1, o_mem):
        return x1 + jnp.dot(o_mem, wmo_ref[...], preferred_element_type=F32)

    def route(rows, x2):
        h3 = _rms(x2, gffn_ref[...])
        h_hi = h3.astype(BF16)
        h_lo = (h3 - h_hi.astype(F32)).astype(BF16)
        both = jnp.dot(h_hi, wr_ref[...], preferred_element_type=F32)
        logits = (both[:, :LANES] + both[:, LANES:]
                  + jnp.dot(h_lo, wr_ref[:, :LANES], preferred_element_type=F32))
        logits = jnp.where(lane < N_EXPERTS, logits, NEG_INF)
        ex = jnp.exp(logits - jnp.max(logits, axis=-1, keepdims=True))
        aff = ex / jnp.sum(ex, axis=-1, keepdims=True)
        xh_ref[rows, 0:D_MODEL] = x2
        xh_ref[rows, D_MODEL:2 * D_MODEL] = h3
        xh_ref[rows, 2 * D_MODEL:] = aff
        afft_ref[:, rows] = aff.T[:N_EXPERTS, :]

    o_pool = [pool(r) for r in parts]
    x1 = [mix(r, o) for r, o in zip(parts, o_pool)]
    q = [mem_query(v) for v in x1]
    o_mem = [mem_attend(v) for v in q]
    x2 = [mem_out(a, o) for a, o in zip(x1, o_mem)]
    for r, v in zip(parts, x2):
        route(r, v)


def _mix(xs, ona, u, mk, mv, wts, t):
    tm = min(256, t)
    nt = t // tm
    hb = tm // POOL_HALO
    last_hb = t // POOL_HALO - 1
    batches = [x.shape[0] for x in xs]
    batch_starts = tuple(sum(batches[:j]) for j in range(len(xs)))
    b = sum(batches)
    n = b * t
    ona3 = ona.reshape(b, t, NA_WIDTH)
    u3 = u.reshape(b, t, POOL_WIDTH)

    def x_spec(start, count):
        def index(bi, i):
            tile = jnp.where(bi < start, 0, jnp.where(bi >= start + count, nt - 1, i))
            return (_held_index(bi, start, count), tile, 0)
        return pl.BlockSpec((1, tm, D_MODEL), index)

    kern = functools.partial(_mix_kernel, tm=tm, seq=t, batch_starts=batch_starts)
    return pl.pallas_call(
        kern,
        grid=(b, nt),
        in_specs=[x_spec(s, c) for s, c in zip(batch_starts, batches)] + [
            pl.BlockSpec((1, tm, NA_WIDTH), lambda bi, i: (bi, i, 0)),
            pl.BlockSpec((1, POOL_HALO, POOL_WIDTH), lambda bi, i: (bi, jnp.maximum(i * hb - 1, 0), 0)),
            pl.BlockSpec((1, tm, POOL_WIDTH), lambda bi, i: (bi, i, 0)),
            pl.BlockSpec((1, POOL_HALO, POOL_WIDTH),
                         lambda bi, i: (bi, jnp.minimum((i + 1) * hb, last_hb), 0)),
            pl.BlockSpec((1, mk.shape[1], MEM_WIDTH), lambda bi, i: (bi, 0, 0)),
            pl.BlockSpec((1, mk.shape[1], MEM_WIDTH), lambda bi, i: (bi, 0, 0)),
            _const_spec((len(POOL_WINDOWS), POOL_GROUP, POOL_GROUP)),
            _const_spec((1, POOL_WIDTH)),
            _const_spec((D_MODEL, D_MODEL)),
            _const_spec((1, D_MODEL)),
            _const_spec((D_MODEL, MEM_WIDTH)),
            _const_spec((MEM_WIDTH, D_MODEL)),
            _const_spec((1, D_MODEL)),
            _const_spec((D_MODEL, 2 * LANES)),
        ],
        out_specs=[pl.BlockSpec((tm, XH_WIDTH), lambda bi, i: (bi * nt + i, 0)),
                   pl.BlockSpec((N_EXPERTS, tm), lambda bi, i: (0, bi * nt + i))],
        out_shape=[jax.ShapeDtypeStruct((n, XH_WIDTH), F32),
                   jax.ShapeDtypeStruct((N_EXPERTS, n), F32)],
        compiler_params=_cparams(("parallel", "parallel")),
        name="mix_mem_router",
    )(*xs, ona3, u3, u3, u3, mk, mv, *wts)


def _total(v):
    return jnp.sum(jnp.sum(v, axis=1, keepdims=True), axis=0, keepdims=True)


def _bits_as_float(bits):
    return pltpu.bitcast(bits, F32)[0:1, 0:1]


def _threshold_kernel(a_ref, thr_ref, *, cap):
    capf = jnp.float32(cap)

    def search(b, los):
        bit = jnp.left_shift(jnp.int32(1), 30 - b)
        out = []
        for e in range(N_EXPERTS):
            cand = los[e] | bit
            cnt = _total(jnp.where(a_ref[e] >= _bits_as_float(cand), 1.0, 0.0))
            out.append(jnp.where(cnt >= capf, cand, los[e]))
        return tuple(out)

    los = lax.fori_loop(0, 31, search, tuple(jnp.zeros((8, LANES), jnp.int32) for _ in range(N_EXPERTS)))
    for e in range(N_EXPERTS):
        thr_ref[e] = los[e]


def _route_kernel(a_ref, thr_ref, idx_ref, *, nch, cap):
    a = a_ref[0]
    capf = jnp.float32(cap)
    total, as_float = _total, _bits_as_float
    thr_bits = thr_ref[0]
    thr = as_float(thr_bits)
    gt = a >= as_float(thr_bits + 1)
    eq = jnp.logical_and(a >= thr, jnp.logical_not(gt))
    need = capf - total(jnp.where(gt, 1.0, 0.0))

    kr = lax.broadcasted_iota(jnp.int32, (LANES, LANES), 0)
    kc = lax.broadcasted_iota(jnp.int32, (LANES, LANES), 1)
    tri_lane = jnp.where(kr <= kc, 1.0, 0.0).astype(BF16)
    cr = lax.broadcasted_iota(jnp.int32, (nch, nch), 0)
    cc = lax.broadcasted_iota(jnp.int32, (nch, nch), 1)
    tri_excl = jnp.where(cc < cr, 1.0, 0.0).astype(BF16)
    tri_incl = jnp.where(cc <= cr, 1.0, 0.0).astype(BF16)

    def prefix(maskf, chunk_tri):
        within = jnp.dot(maskf.astype(BF16), tri_lane, preferred_element_type=F32)
        tot = jnp.broadcast_to(within[:, LANES - 1:LANES], (nch, LANES))
        before = jnp.dot(chunk_tri, tot.astype(BF16), preferred_element_type=F32)
        return within, before

    eqf = jnp.where(eq, 1.0, 0.0)
    eq_within, eq_before = prefix(eqf, tri_excl)
    eq_rank = eq_before + eq_within - eqf
    self_ = jnp.where(gt, 1.0, jnp.where(eq, jnp.where(eq_rank < need, 1.0, 0.0), 0.0))

    within, cs_incl = prefix(self_, tri_incl)
    cs_col = cs_incl[:, 0:1]
    j = lax.broadcasted_iota(jnp.int32, (1, cap), 1).astype(F32)
    done = cs_col <= j
    chunk = jnp.sum(jnp.where(done, 1.0, 0.0), axis=0, keepdims=True)
    cs_prev = jnp.max(jnp.where(done, cs_col, 0.0), axis=0, keepdims=True)
    crow = lax.broadcasted_iota(jnp.int32, (nch, cap), 0).astype(F32)
    onehot = jnp.where(crow == chunk, 1.0, 0.0).astype(BF16)
    wsel = jnp.dot(within.T.astype(BF16), onehot, preferred_element_type=F32)
    rank = j - cs_prev
    inner = jnp.sum(jnp.where(wsel <= rank, 1.0, 0.0), axis=0, keepdims=True)
    idx_ref[0] = (chunk * LANES + inner).astype(jnp.int32)


def _route(aff_t, row0):
    n = aff_t.shape[1]
    cap = EC_CAPACITY_FACTOR * n // N_EXPERTS
    nch = n // LANES
    aff3 = aff_t.reshape(N_EXPERTS, nch, LANES)
    thr = pl.pallas_call(
        functools.partial(_threshold_kernel, cap=cap),
        grid=(1,),
        in_specs=[pl.BlockSpec((N_EXPERTS, nch, LANES), lambda i: (0, 0, 0))],
        out_specs=pl.BlockSpec((N_EXPERTS, 8, LANES), lambda i: (0, 0, 0)),
        out_shape=jax.ShapeDtypeStruct((N_EXPERTS, 8, LANES), jnp.int32),
        compiler_params=_cparams(("arbitrary",)),
        name="route_threshold",
    )(aff3)
    kern = functools.partial(_route_kernel, nch=nch, cap=cap)
    idx = pl.pallas_call(
        kern,
        grid=(N_EXPERTS,),
        in_specs=[pl.BlockSpec((1, nch, LANES), lambda e: (e, 0, 0)),
                  pl.BlockSpec((1, 8, LANES), lambda e: (e, 0, 0))],
        out_specs=pl.BlockSpec((1, 1, cap), lambda e: (e, 0, 0)),
        out_shape=jax.ShapeDtypeStruct((N_EXPERTS, 1, cap), jnp.int32),
        compiler_params=_cparams(("parallel",)),
        name="route",
    )(aff3, thr)
    return idx.reshape(N_EXPERTS, cap) + row0


def _moe_kernel(idx_ref, xh_in_ref, wg_hbm, wu_hbm, wd_hbm, xh_ref, wbf_hbm,
                wg_ref, wu_ref, wd_ref, wstage, bstage, xbuf, obuf, hid_ref, gsem, ssem, wsem, bsem, lsem,
                *, tm, tiles):
    del xh_in_ref
    e = pl.program_id(0)
    i = pl.program_id(1)
    s = e * tiles + i
    last = N_EXPERTS * tiles - 1
    slot = lax.rem(s, 2)
    other = 1 - slot
    base = s * tm
    mats = ((wg_hbm, wg_ref), (wu_hbm, wu_ref), (wd_hbm, wd_ref))

    def gather_copy(tok, r, sl):
        return pltpu.make_async_copy(xh_ref.at[pl.ds(tok, 1), :], xbuf.at[sl, pl.ds(r, 1), :], gsem.at[sl])

    def scatter_copy(tok, r, sl):
        return pltpu.make_async_copy(obuf.at[sl, pl.ds(r, 1), :],
                                     xh_ref.at[pl.ds(tok, 1), pl.ds(0, D_MODEL)], ssem.at[sl])

    def start_rows(copy, first, sl):
        def body(r, c):
            copy(idx_ref[first + r], r, sl).start()
            return c
        lax.fori_loop(0, tm, body, 0, unroll=8)

    def wait_rows(copy, sl):
        for _ in range(tm):
            copy(0, 0, sl).wait()

    def load_weights():
        jobs = [(src, dst, c) for src, dst in mats for c in range(W_CHUNKS)]

        def chunk_copy(k):
            src, _, c = jobs[k]
            return pltpu.make_async_copy(src.at[e, pl.ds(c * W_ROWS, W_ROWS), :], wstage.at[k % 2], wsem.at[k % 2])

        chunk_copy(0).start()
        for k, (_, dst, c) in enumerate(jobs):
            if k + 1 < len(jobs):
                chunk_copy(k + 1).start()
            chunk_copy(k).wait()
            dst[c * W_ROWS:(c + 1) * W_ROWS, :] = wstage[k % 2].astype(BF16)

    active = 1 << (min(tiles, W_CHUNKS).bit_length() - 1)
    per_step = W_CHUNKS // active
    nxt = lax.rem(e + 1, N_EXPERTS)
    conv_jobs = [(m, j) for m in range(len(mats)) for j in range(per_step)]

    def conv_row(tile, j):
        return pl.multiple_of((lax.rem(tile, active) * per_step + j) * W_ROWS, W_ROWS)

    def conv_in(m, j, tile):
        return pltpu.make_async_copy(mats[m][0].at[nxt, pl.ds(conv_row(tile, j), W_ROWS), :],
                                     wstage.at[m * per_step + j], wsem.at[0])

    def conv_out(m, j):
        return pltpu.make_async_copy(bstage.at[m * per_step + j],
                                     wbf_hbm.at[nxt, pl.ds(m * D_MODEL + conv_row(i, j), W_ROWS), :], bsem.at[0])

    def weight_load(m, expert):
        return pltpu.make_async_copy(wbf_hbm.at[expert, pl.ds(m * D_MODEL, D_MODEL), :], mats[m][1], lsem.at[0])

    @pl.when(s >= 1)
    def _():
        wait_rows(scatter_copy, slot)

    @pl.when(i >= 1)
    def _():
        for m, j in conv_jobs:
            conv_out(m, j).wait()

    @pl.when(s == 0)
    def _():
        start_rows(gather_copy, base, slot)

    @pl.when(i == 0)
    def _():
        @pl.when(e == 0)
        def _():
            load_weights()

        @pl.when(e >= 1)
        def _():
            for m in range(len(mats)):
                weight_load(m, e).wait()

        for m, j in conv_jobs:
            conv_in(m, j, i).start()

    wait_rows(gather_copy, slot)
    for m, j in conv_jobs:
        conv_in(m, j, i).wait()
    for m, j in conv_jobs:
        bstage[m * per_step + j] = wstage[m * per_step + j].astype(BF16)
        conv_out(m, j).start()

    @pl.when(s == 0)
    def _():
        obuf[other] = xbuf[slot, :, 0:D_MODEL]

    wb_first = jnp.where(s == 0, base, base - tm)
    pf_first = jnp.where(s < last, base + tm, base)
    rows_per_chunk = tm // MOE_CHUNKS
    fw = D_MODEL // MOE_CHUNKS
    xs = xbuf[slot, :, D_MODEL:2 * D_MODEL].astype(BF16)
    for c in range(MOE_CHUNKS):
        a = jnp.dot(xs, wg_ref[:, c * fw:(c + 1) * fw], preferred_element_type=F32)
        up = jnp.dot(xs, wu_ref[:, c * fw:(c + 1) * fw], preferred_element_type=F32)
        hid_ref[:, c * fw:(c + 1) * fw] = (a * jax.nn.sigmoid(a) * up).astype(BF16)
        for r in range(c * rows_per_chunk, (c + 1) * rows_per_chunk):
            scatter_copy(idx_ref[wb_first + r], r, other).start()
            gather_copy(idx_ref[pf_first + r], r, other).start()
    y = jnp.dot(hid_ref[...], wd_ref[...], preferred_element_type=F32)
    lane = lax.broadcasted_iota(jnp.int32, (tm, LANES), 1)
    gate = jnp.sum(jnp.where(lane == e, xbuf[slot, :, 2 * D_MODEL:], 0.0), axis=1, keepdims=True)
    obuf[slot] = xbuf[slot, :, 0:D_MODEL] + y * gate

    @pl.when(i + 1 < tiles)
    def _():
        for m, j in conv_jobs:
            conv_in(m, j, i + 1).start()

    @pl.when(i == tiles - 1)
    def _():
        for m, j in conv_jobs:
            conv_out(m, j).wait()

        @pl.when(e + 1 < N_EXPERTS)
        def _():
            for m in range(len(mats)):
                weight_load(m, e + 1).start()

    @pl.when(s == last)
    def _():
        wait_rows(gather_copy, other)
        wait_rows(scatter_copy, other)
        start_rows(scatter_copy, base, slot)
        wait_rows(scatter_copy, slot)


def _moe(idx_parts, xh, w_gate, w_up, w_down, tm):
    assert len(idx_parts) >= 2 and all(a.shape[1] % tm == 0 and a.shape[1] // tm >= 2 for a in idx_parts)
    idx = jnp.concatenate(idx_parts, axis=1)
    cap = idx.shape[1]
    tiles = cap // tm
    assert tm % MOE_CHUNKS == 0
    per_step = W_CHUNKS // (1 << (min(tiles, W_CHUNKS).bit_length() - 1))
    kern = functools.partial(_moe_kernel, tm=tm, tiles=tiles)
    any_spec = pl.BlockSpec(memory_space=pl.ANY)
    xh, _ = pl.pallas_call(
        kern,
        grid_spec=pltpu.PrefetchScalarGridSpec(
            num_scalar_prefetch=1,
            grid=(N_EXPERTS, tiles),
            in_specs=[any_spec, any_spec, any_spec, any_spec],
            out_specs=[any_spec, any_spec],
            scratch_shapes=[pltpu.VMEM((D_MODEL, D_MODEL), BF16),
                            pltpu.VMEM((D_MODEL, D_MODEL), BF16),
                            pltpu.VMEM((D_MODEL, D_MODEL), BF16),
                            pltpu.VMEM((max(2, 3 * per_step), W_ROWS, D_MODEL), F32),
                            pltpu.VMEM((3 * per_step, W_ROWS, D_MODEL), BF16),
                            pltpu.VMEM((2, tm, XH_WIDTH), F32),
                            pltpu.VMEM((2, tm, D_MODEL), F32),
                            pltpu.VMEM((tm, D_MODEL), BF16),
                            pltpu.SemaphoreType.DMA((2,)),
                            pltpu.SemaphoreType.DMA((2,)),
                            pltpu.SemaphoreType.DMA((2,)),
                            pltpu.SemaphoreType.DMA((1,)),
                            pltpu.SemaphoreType.DMA((1,))],
        ),
        out_shape=[jax.ShapeDtypeStruct(xh.shape, F32),
                   jax.ShapeDtypeStruct((N_EXPERTS, 3 * D_MODEL, D_MODEL), BF16)],
        input_output_aliases={1: 0},
        compiler_params=pltpu.CompilerParams(dimension_semantics=("arbitrary", "arbitrary"),
                                             vmem_limit_bytes=VMEM_LIMIT, disable_bounds_checks=True),
        name="moe_ffn",
    )(idx.reshape(-1), xh, w_gate, w_up, w_down)
    return xh


def _final_kernel(x_ref, g_ref, o_ref):
    o_ref[...] = _rms(x_ref[...], g_ref[...])


def _final_norm(xh, g_final, row0, n):
    tm = min(512, n)
    blk0 = row0 // tm
    assert row0 % tm == 0
    return pl.pallas_call(
        _final_kernel,
        grid=(n // tm,),
        in_specs=[pl.BlockSpec((tm, D_MODEL), lambda i: (blk0 + i, 0)), _const_spec((1, D_MODEL))],
        out_specs=pl.BlockSpec((tm, D_MODEL), lambda i: (i, 0)),
        out_shape=jax.ShapeDtypeStruct((n, D_MODEL), F32),
        compiler_params=_cparams(("parallel",)),
        name="final_norm",
    )(xh, g_final.reshape(1, D_MODEL))


def _forward(xs, mems, p, moe_tm=256):
    t = xs[0].shape[1]
    assert all(x.shape[1] == t for x in xs)
    sizes = [x.shape[0] * t for x in xs]
    starts = [sum(sizes[:j]) for j in range(len(xs))]
    b = sum(x.shape[0] for x in xs)
    wts = (p["w_pool"], p["pool_scale"], p["w_out"], p["g_mq"], p["w_mq"], p["w_mo"], p["g_ffn"],
           p["w_router"])
    q, k, v, u = _in_proj([x.reshape(-1, D_MODEL) for x in xs], p["g_mix"], p["w_in"])
    ona = _na_attention(q, k, v, p["na_bias"], b, t)
    mk, mv = _mem_kv(jnp.concatenate(mems, axis=0), p["g_mkv"], p["w_mk"], p["w_mv"])
    xh, aff_t = _mix(xs, ona, u, mk, mv, wts, t)
    idx = [_route(aff_t[:, row0:row0 + n], row0) for row0, n in zip(starts, sizes)]
    xh = _moe(idx, xh, p["w_gate"], p["w_up"], p["w_down"], moe_tm)
    return tuple(_final_norm(xh, p["g_final"], row0, n).reshape(x.shape)
                 for x, row0, n in zip(xs, starts, sizes))


def _prepare(g_mix, w_in, rpb, w_pool, pool_scale, w_out, g_mq, g_mkv, w_mq, w_mk, w_mv, w_mo,
             g_ffn, w_router, w_gate, w_up, w_down, g_final):
    wr = jnp.pad(w_router[0], ((0, 0), (0, LANES - N_EXPERTS)))
    wr_hi = wr.astype(BF16)
    wr_lo = (wr - wr_hi.astype(F32)).astype(BF16)
    row = lambda a: a.reshape(1, -1)
    return dict(
        g_mix=g_mix[0], w_in=w_in[0].astype(BF16), na_bias=_na_bias_table(rpb[0]),
        w_pool=w_pool[0].astype(BF16), pool_scale=row(pool_scale[0]), w_out=w_out[0].astype(BF16),
        g_mq=row(g_mq[0]), g_mkv=g_mkv[0], w_mq=w_mq[0].astype(BF16), w_mk=w_mk[0].astype(BF16),
        w_mv=w_mv[0].astype(BF16), w_mo=w_mo[0].astype(BF16), g_ffn=row(g_ffn[0]),
        w_router=jnp.concatenate([wr_hi, wr_lo], axis=1),
        w_gate=w_gate[0], w_up=w_up[0], w_down=w_down[0], g_final=g_final)


def kernel(x_prompt, x_sample, mem_prompt, mem_sample, g_mix, w_in, rpb, w_pool, pool_scale, w_out,
           g_mq, g_mkv, w_mq, w_mk, w_mv, w_mo, g_ffn, w_router, w_gate, w_up, w_down, g_final):
    p = _prepare(g_mix, w_in, rpb, w_pool, pool_scale, w_out, g_mq, g_mkv, w_mq, w_mk, w_mv, w_mo,
                 g_ffn, w_router, w_gate, w_up, w_down, g_final)
    return _forward((x_prompt, x_sample), (mem_prompt, mem_sample), p)
```

```python
import functools

import jax
import jax.numpy as jnp
from jax import lax
from jax.experimental import pallas as pl
from jax.experimental.pallas import tpu as pltpu

F32 = jnp.float32
BF16 = jnp.bfloat16

D_MODEL = 2048
GRID_W = 64
NA_WIDTH = 1024
NA_HEAD_DIM = 64
NA_HEADS = 16
WIN_ROWS = 8
WIN_COLS = 16
POOL_WINDOWS = (2, 4, 8, 16)
POOL_WIDTH = 1024
POOL_GROUP = 256
MEM_HEADS = 4
MEM_HEAD_DIM = 128
MEM_WIDTH = 512
N_EXPERTS = 16
EC_CAPACITY_FACTOR = 2
EPS = 1e-6
NEG_INF = -1e30
LOG2_E = 1.4426950408889634

LANES = 128
POOL_HALO = 16
XH_WIDTH = 2 * D_MODEL + LANES
VMEM_LIMIT = 56 * 1024 * 1024
MOE_CHUNKS = 8
MOE_QUIET_CHUNKS = 2
W_LOAD_PARTS = 4
W_ROWS = 128
W_CHUNKS = D_MODEL // W_ROWS
NA_ROWS = 32
NA_GROUP = 2
MIX_PARTS = 2


def _cparams(sem, vmem=VMEM_LIMIT):
    return pltpu.CompilerParams(dimension_semantics=sem, vmem_limit_bytes=vmem)


def _const_spec(shape):
    nd = len(shape)
    return pl.BlockSpec(shape, lambda *_: (0,) * nd, pipeline_mode=pl.Buffered(1))


def _rms(v, g):
    return v * lax.rsqrt(jnp.mean(v * v, axis=-1, keepdims=True) + EPS) * g


def _pick(step, starts, refs, read):
    val = read(refs[0])
    for start, ref in zip(starts[1:], refs[1:]):
        val = jnp.where(step >= start, read(ref), val)
    return val


def _held_index(step, start, count):
    return jnp.clip(step - start, 0, count - 1)


def _in_proj_kernel(*refs, starts):
    n_src = len(starts)
    x_refs, (g_ref, w_ref), outs = refs[:n_src], refs[n_src:n_src + 2], refs[n_src + 2:]
    x = _pick(pl.program_id(0), starts, x_refs, lambda r: r[...])
    h = _rms(x, g_ref[...]).astype(BF16)
    for j, o_ref in enumerate(outs):
        o_ref[...] = jnp.dot(h, w_ref[:, j * NA_WIDTH:(j + 1) * NA_WIDTH],
                             preferred_element_type=F32).astype(BF16)


def _in_proj(x2ds, g_mix, w_in_bf):
    tm = min(512, *(x.shape[0] for x in x2ds))
    counts = [x.shape[0] // tm for x in x2ds]
    starts = tuple(sum(counts[:j]) for j in range(len(x2ds)))
    n = sum(x.shape[0] for x in x2ds)
    out = jax.ShapeDtypeStruct((n, NA_WIDTH), BF16)
    row = lambda i: (i, 0)
    x_specs = [pl.BlockSpec((tm, D_MODEL), functools.partial(lambda i, s, c: (_held_index(i, s, c), 0), s=s, c=c))
               for s, c in zip(starts, counts)]
    return pl.pallas_call(
        functools.partial(_in_proj_kernel, starts=starts),
        grid=(n // tm,),
        in_specs=x_specs + [_const_spec((1, D_MODEL)), _const_spec((D_MODEL, 4 * NA_WIDTH))],
        out_specs=[pl.BlockSpec((tm, NA_WIDTH), row)] * 4,
        out_shape=[out] * 4,
        compiler_params=_cparams(("parallel",)),
        name="in_proj",
    )(*x2ds, g_mix.reshape(1, D_MODEL), w_in_bf)


def _na_kernel(q_ref, k_ref, v_ref, b_ref, o_ref, *, rows_per_step, n_rows):
    rb = pl.program_id(2)
    lane = lax.broadcasted_iota(jnp.int32, (1, LANES), 1)
    m0 = jnp.where(lane < NA_HEAD_DIM, 1.0, 0.0).astype(BF16)
    m1 = jnp.where(lane >= NA_HEAD_DIM, 1.0, 0.0).astype(BF16)
    lane_f = lax.broadcasted_iota(jnp.int32, (GRID_W, LANES), 1)
    scale = NA_HEAD_DIM ** -0.5 * LOG2_E

    def score_stage(g):
        out = []
        for j in range(NA_GROUP):
            i = g * NA_GROUP + j
            r = rb * rows_per_step + i
            r0 = jnp.clip(r - WIN_ROWS // 2, 0, n_rows - WIN_ROWS)
            qr = q_ref[0, i * GRID_W:(i + 1) * GRID_W, :]
            qq = jnp.concatenate([qr * m0, qr * m1], axis=0)
            kstart = pl.multiple_of(r0 * GRID_W, GRID_W)
            kw = k_ref[0, pl.ds(kstart, WIN_ROWS * GRID_W), :]
            s = lax.dot_general(qq, kw, (((1,), (1,)), ((), ())), preferred_element_type=F32)
            out.append((i, r - r0, kstart, s))
        return out

    def softmax_stage(rows):
        out = []
        for i, d, kstart, s in rows:
            s = s * scale + b_ref[0, d]
            p = jnp.exp2(s - jnp.max(s, axis=-1, keepdims=True))
            out.append((i, kstart, p.astype(BF16), jnp.sum(p, axis=-1, keepdims=True)))
        return out

    def value_stage(rows):
        for i, kstart, p, l in rows:
            vw = v_ref[0, pl.ds(kstart, WIN_ROWS * GRID_W), :]
            o = jnp.dot(p, vw, preferred_element_type=F32) / l
            out = jnp.where(lane_f < NA_HEAD_DIM, o[:GRID_W], o[GRID_W:])
            o_ref[0, i * GRID_W:(i + 1) * GRID_W, :] = out.astype(BF16)

    n_groups = rows_per_step // NA_GROUP
    scored, soft = None, None
    for g in range(n_groups + 2):
        nxt_scored = score_stage(g) if g < n_groups else None
        nxt_soft = softmax_stage(scored) if scored is not None else None
        if soft is not None:
            value_stage(soft)
        scored, soft = nxt_scored, nxt_soft


def _na_bias_table(rpb):
    cols = jnp.arange(GRID_W)
    c0 = jnp.clip(cols - WIN_COLS // 2, 0, GRID_W - WIN_COLS)
    col_mask = (cols[None, :] >= c0[:, None]) & (cols[None, :] < c0[:, None] + WIN_COLS)
    dc_idx = jnp.clip(cols[None, :] - cols[:, None], -(WIN_COLS - 1), WIN_COLS - 1) + WIN_COLS - 1
    onehot = (dc_idx[:, :, None] == jnp.arange(2 * WIN_COLS - 1)).astype(F32)
    t = jnp.einsum('hrc,qkc->hqrk', rpb.astype(F32) * LOG2_E, onehot, precision=lax.Precision.HIGHEST)
    t = jnp.where(col_mask[None, :, None, :], t, NEG_INF)
    per_d = [t[:, :, WIN_ROWS - 1 - d:2 * WIN_ROWS - 1 - d, :]
             .reshape(NA_HEADS // 2, 2 * GRID_W, WIN_ROWS * GRID_W) for d in range(WIN_ROWS)]
    return jnp.stack(per_d, axis=1)


def _na_attention(q, k, v, bias, b, t):
    n_rows = t // GRID_W
    assert n_rows >= WIN_ROWS
    rows_per_step = min(NA_ROWS, n_rows)
    assert rows_per_step % NA_GROUP == 0 and n_rows % rows_per_step == 0
    tq = rows_per_step * GRID_W
    q3, k3, v3 = (a.reshape(b, t, NA_WIDTH) for a in (q, k, v))
    n_pairs = NA_HEADS // 2
    kern = functools.partial(_na_kernel, rows_per_step=rows_per_step, n_rows=n_rows)
    out = pl.pallas_call(
        kern,
        grid=(b, n_pairs, n_rows // rows_per_step),
        in_specs=[pl.BlockSpec((1, tq, LANES), lambda bi, p, r: (bi, r, p)),
                  pl.BlockSpec((1, t, LANES), lambda bi, p, r: (bi, 0, p)),
                  pl.BlockSpec((1, t, LANES), lambda bi, p, r: (bi, 0, p)),
                  pl.BlockSpec((1, WIN_ROWS, 2 * GRID_W, WIN_ROWS * GRID_W),
                               lambda bi, p, r: (p, 0, 0, 0))],
        out_specs=pl.BlockSpec((1, tq, LANES), lambda bi, p, r: (bi, r, p)),
        out_shape=jax.ShapeDtypeStruct((b, t, NA_WIDTH), BF16),
        compiler_params=_cparams(("parallel", "parallel", "parallel")),
        name="na_attention",
    )(q3, k3, v3, bias)
    return out.reshape(b * t, NA_WIDTH)


def _mem_kv_kernel(m_ref, g_ref, wk_ref, wv_ref, k_ref, v_ref):
    h = _rms(m_ref[0], g_ref[...]).astype(BF16)
    k_ref[0] = jnp.dot(h, wk_ref[...], preferred_element_type=F32).astype(BF16)
    v_ref[0] = jnp.dot(h, wv_ref[...], preferred_element_type=F32).astype(BF16)


def _mem_kv(mem, g_mkv, w_mk_bf, w_mv_bf):
    b, m, _ = mem.shape
    out = jax.ShapeDtypeStruct((b, m, MEM_WIDTH), BF16)
    return pl.pallas_call(
        _mem_kv_kernel,
        grid=(b,),
        in_specs=[pl.BlockSpec((1, m, D_MODEL), lambda i: (i, 0, 0)),
                  _const_spec((1, D_MODEL)),
                  _const_spec((D_MODEL, MEM_WIDTH)),
                  _const_spec((D_MODEL, MEM_WIDTH))],
        out_specs=[pl.BlockSpec((1, m, MEM_WIDTH), lambda i: (i, 0, 0))] * 2,
        out_shape=[out] * 2,
        compiler_params=_cparams(("parallel",)),
        name="mem_kv",
    )(mem, g_mkv.reshape(1, D_MODEL), w_mk_bf, w_mv_bf)


def _mix_kernel(*refs, tm, seq, batch_starts):
    x_refs = refs[:len(batch_starts)]
    (ona_ref, up_ref, uc_ref, un_ref, mk_ref, mv_ref, wpool_ref, pscale_ref, wout_ref, gmq_ref,
     wmq_ref, wmo_ref, gffn_ref, wr_ref, xh_ref, afft_ref) = refs[len(batch_starts):]
    t0 = pl.program_id(1) * tm
    pm = tm // MIX_PARTS
    width = pm + 2 * POOL_HALO
    ucat = jnp.concatenate([up_ref[0], uc_ref[0], un_ref[0]], axis=0)
    mk = mk_ref[0]
    mv = mv_ref[0]
    lane = lax.broadcasted_iota(jnp.int32, (pm, LANES), 1)
    parts = [slice(j * pm, (j + 1) * pm) for j in range(MIX_PARTS)]

    def pool(rows):
        p0 = t0 + rows.start
        useg = ucat[rows.start:rows.start + width]
        pos = p0 + lax.broadcasted_iota(jnp.int32, (pm, width), 0)
        src = p0 - POOL_HALO + lax.broadcasted_iota(jnp.int32, (pm, width), 1)
        pcol = p0 + lax.broadcasted_iota(jnp.int32, (pm, 1), 0)
        pools = []
        for g, w in enumerate(POOL_WINDOWS):
            sl = slice(g * POOL_GROUP, (g + 1) * POOL_GROUP)
            lo = jnp.maximum(pos - w // 2, 0)
            hi = jnp.minimum(pos + w // 2, seq)
            band = jnp.where(src >= lo, jnp.where(src < hi, 1.0, 0.0), 0.0).astype(BF16)
            wsum = jnp.dot(band, useg[:, sl], preferred_element_type=F32)
            cnt = (jnp.minimum(pcol + w // 2, seq) - jnp.maximum(pcol - w // 2, 0)).astype(F32)
            dlt = (wsum / cnt - useg[POOL_HALO:POOL_HALO + pm, sl].astype(F32)).astype(BF16)
            pools.append(jnp.dot(dlt, wpool_ref[g], preferred_element_type=F32))
        return (jnp.concatenate(pools, axis=1) * pscale_ref[...]).astype(BF16)

    def mix(rows, o_pool):
        x = _pick(pl.program_id(0), batch_starts, x_refs, lambda r: r[0, rows, :])
        return (x
                + jnp.dot(ona_ref[0, rows, :], wout_ref[:NA_WIDTH, :], preferred_element_type=F32)
                + jnp.dot(o_pool, wout_ref[NA_WIDTH:, :], preferred_element_type=F32))

    def mem_query(x1):
        hq = _rms(x1, gmq_ref[...]).astype(BF16)
        return jnp.dot(hq, wmq_ref[...], preferred_element_type=F32).astype(BF16)

    def mem_attend(q):
        heads = []
        for h in range(MEM_HEADS):
            sl = slice(h * MEM_HEAD_DIM, (h + 1) * MEM_HEAD_DIM)
            s = lax.dot_general(q[:, sl], mk[:, sl], (((1,), (1,)), ((), ())),
                                preferred_element_type=F32) * (MEM_HEAD_DIM ** -0.5)
            p = jnp.exp(s - jnp.max(s, axis=-1, keepdims=True))
            l = jnp.sum(p, axis=-1, keepdims=True)
            heads.append(jnp.dot(p.astype(BF16), mv[:, sl], preferred_element_type=F32) / l)
        return jnp.concatenate(heads, axis=1).astype(BF16)

    def mem_out(x1, o_mem):
        return x1 + jnp.dot(o_mem, wmo_ref[...], preferred_element_type=F32)

    def route(rows, x2):
        h3 = _rms(x2, gffn_ref[...])
        h_hi = h3.astype(BF16)
        h_lo = (h3 - h_hi.astype(F32)).astype(BF16)
        both = jnp.dot(h_hi, wr_ref[...], preferred_element_type=F32)
        logits = (both[:, :LANES] + both[:, LANES:]
                  + jnp.dot(h_lo, wr_ref[:, :LANES], preferred_element_type=F32))
        logits = jnp.where(lane < N_EXPERTS, logits, NEG_INF)
        ex = jnp.exp(logits - jnp.max(logits, axis=-1, keepdims=True))
        aff = ex / jnp.sum(ex, axis=-1, keepdims=True)
        xh_ref[rows, 0:D_MODEL] = x2
        xh_ref[rows, D_MODEL:2 * D_MODEL] = h3
        xh_ref[rows, 2 * D_MODEL:] = aff
        afft_ref[:, rows] = aff.T[:N_EXPERTS, :]

    o_pool = [pool(r) for r in parts]
    x1 = [mix(r, o) for r, o in zip(parts, o_pool)]
    q = [mem_query(v) for v in x1]
    o_mem = [mem_attend(v) for v in q]
    x2 = [mem_out(a, o) for a, o in zip(x1, o_mem)]
    for r, v in zip(parts, x2):
        route(r, v)


def _mix(xs, ona, u, mk, mv, wts, t):
    tm = min(256, t)
    nt = t // tm
    hb = tm // POOL_HALO
    last_hb = t // POOL_HALO - 1
    batches = [x.shape[0] for x in xs]
    batch_starts = tuple(sum(batches[:j]) for j in range(len(xs)))
    b = sum(batches)
    n = b * t
    ona3 = ona.reshape(b, t, NA_WIDTH)
    u3 = u.reshape(b, t, POOL_WIDTH)

    def x_spec(start, count):
        def index(bi, i):
            tile = jnp.where(bi < start, 0, jnp.where(bi >= start + count, nt - 1, i))
            return (_held_index(bi, start, count), tile, 0)
        return pl.BlockSpec((1, tm, D_MODEL), index)

    kern = functools.partial(_mix_kernel, tm=tm, seq=t, batch_starts=batch_starts)
    return pl.pallas_call(
        kern,
        grid=(b, nt),
        in_specs=[x_spec(s, c) for s, c in zip(batch_starts, batches)] + [
            pl.BlockSpec((1, tm, NA_WIDTH), lambda bi, i: (bi, i, 0)),
            pl.BlockSpec((1, POOL_HALO, POOL_WIDTH), lambda bi, i: (bi, jnp.maximum(i * hb - 1, 0), 0)),
            pl.BlockSpec((1, tm, POOL_WIDTH), lambda bi, i: (bi, i, 0)),
            pl.BlockSpec((1, POOL_HALO, POOL_WIDTH),
                         lambda bi, i: (bi, jnp.minimum((i + 1) * hb, last_hb), 0)),
            pl.BlockSpec((1, mk.shape[1], MEM_WIDTH), lambda bi, i: (bi, 0, 0)),
            pl.BlockSpec((1, mk.shape[1], MEM_WIDTH), lambda bi, i: (bi, 0, 0)),
            _const_spec((len(POOL_WINDOWS), POOL_GROUP, POOL_GROUP)),
            _const_spec((1, POOL_WIDTH)),
            _const_spec((D_MODEL, D_MODEL)),
            _const_spec((1, D_MODEL)),
            _const_spec((D_MODEL, MEM_WIDTH)),
            _const_spec((MEM_WIDTH, D_MODEL)),
            _const_spec((1, D_MODEL)),
            _const_spec((D_MODEL, 2 * LANES)),
        ],
        out_specs=[pl.BlockSpec((tm, XH_WIDTH), lambda bi, i: (bi * nt + i, 0)),
                   pl.BlockSpec((N_EXPERTS, tm), lambda bi, i: (0, bi * nt + i))],
        out_shape=[jax.ShapeDtypeStruct((n, XH_WIDTH), F32),
                   jax.ShapeDtypeStruct((N_EXPERTS, n), F32)],
        compiler_params=_cparams(("parallel", "parallel")),
        name="mix_mem_router",
    )(*xs, ona3, u3, u3, u3, mk, mv, *wts)


def _total(v):
    return jnp.sum(jnp.sum(v, axis=1, keepdims=True), axis=0, keepdims=True)


def _bits_as_float(bits):
    return pltpu.bitcast(bits, F32)[0:1, 0:1]


def _threshold_kernel(a_ref, thr_ref, *, cap):
    capf = jnp.float32(cap)

    def search(b, los):
        bit = jnp.left_shift(jnp.int32(1), 30 - b)
        out = []
        for e in range(N_EXPERTS):
            cand = los[e] | bit
            cnt = _total(jnp.where(a_ref[e] >= _bits_as_float(cand), 1.0, 0.0))
            out.append(jnp.where(cnt >= capf, cand, los[e]))
        return tuple(out)

    los = lax.fori_loop(0, 31, search, tuple(jnp.zeros((8, LANES), jnp.int32) for _ in range(N_EXPERTS)))
    for e in range(N_EXPERTS):
        thr_ref[e] = los[e]


def _route_kernel(a_ref, thr_ref, idx_ref, *, nch, cap):
    a = a_ref[0]
    capf = jnp.float32(cap)
    total, as_float = _total, _bits_as_float
    thr_bits = thr_ref[0]
    thr = as_float(thr_bits)
    gt = a >= as_float(thr_bits + 1)
    eq = jnp.logical_and(a >= thr, jnp.logical_not(gt))
    need = capf - total(jnp.where(gt, 1.0, 0.0))

    kr = lax.broadcasted_iota(jnp.int32, (LANES, LANES), 0)
    kc = lax.broadcasted_iota(jnp.int32, (LANES, LANES), 1)
    tri_lane = jnp.where(kr <= kc, 1.0, 0.0).astype(BF16)
    cr = lax.broadcasted_iota(jnp.int32, (nch, nch), 0)
    cc = lax.broadcasted_iota(jnp.int32, (nch, nch), 1)
    tri_excl = jnp.where(cc < cr, 1.0, 0.0).astype(BF16)
    tri_incl = jnp.where(cc <= cr, 1.0, 0.0).astype(BF16)

    def prefix(maskf, chunk_tri):
        within = jnp.dot(maskf.astype(BF16), tri_lane, preferred_element_type=F32)
        tot = jnp.broadcast_to(within[:, LANES - 1:LANES], (nch, LANES))
        before = jnp.dot(chunk_tri, tot.astype(BF16), preferred_element_type=F32)
        return within, before

    eqf = jnp.where(eq, 1.0, 0.0)
    eq_within, eq_before = prefix(eqf, tri_excl)
    eq_rank = eq_before + eq_within - eqf
    self_ = jnp.where(gt, 1.0, jnp.where(eq, jnp.where(eq_rank < need, 1.0, 0.0), 0.0))

    within, cs_incl = prefix(self_, tri_incl)
    cs_col = cs_incl[:, 0:1]
    j = lax.broadcasted_iota(jnp.int32, (1, cap), 1).astype(F32)
    done = cs_col <= j
    chunk = jnp.sum(jnp.where(done, 1.0, 0.0), axis=0, keepdims=True)
    cs_prev = jnp.max(jnp.where(done, cs_col, 0.0), axis=0, keepdims=True)
    crow = lax.broadcasted_iota(jnp.int32, (nch, cap), 0).astype(F32)
    onehot = jnp.where(crow == chunk, 1.0, 0.0).astype(BF16)
    wsel = jnp.dot(within.T.astype(BF16), onehot, preferred_element_type=F32)
    rank = j - cs_prev
    inner = jnp.sum(jnp.where(wsel <= rank, 1.0, 0.0), axis=0, keepdims=True)
    idx_ref[0] = (chunk * LANES + inner).astype(jnp.int32)


def _route(aff_t, row0):
    n = aff_t.shape[1]
    cap = EC_CAPACITY_FACTOR * n // N_EXPERTS
    nch = n // LANES
    aff3 = aff_t.reshape(N_EXPERTS, nch, LANES)
    thr = pl.pallas_call(
        functools.partial(_threshold_kernel, cap=cap),
        grid=(1,),
        in_specs=[pl.BlockSpec((N_EXPERTS, nch, LANES), lambda i: (0, 0, 0))],
        out_specs=pl.BlockSpec((N_EXPERTS, 8, LANES), lambda i: (0, 0, 0)),
        out_shape=jax.ShapeDtypeStruct((N_EXPERTS, 8, LANES), jnp.int32),
        compiler_params=_cparams(("arbitrary",)),
        name="route_threshold",
    )(aff3)
    kern = functools.partial(_route_kernel, nch=nch, cap=cap)
    idx = pl.pallas_call(
        kern,
        grid=(N_EXPERTS,),
        in_specs=[pl.BlockSpec((1, nch, LANES), lambda e: (e, 0, 0)),
                  pl.BlockSpec((1, 8, LANES), lambda e: (e, 0, 0))],
        out_specs=pl.BlockSpec((1, 1, cap), lambda e: (e, 0, 0)),
        out_shape=jax.ShapeDtypeStruct((N_EXPERTS, 1, cap), jnp.int32),
        compiler_params=_cparams(("parallel",)),
        name="route",
    )(aff3, thr)
    return idx.reshape(N_EXPERTS, cap) + row0


def _moe_kernel(idx_ref, xh_in_ref, wg_hbm, wu_hbm, wd_hbm, xh_ref, wbf_hbm,
                wg_ref, wu_ref, wd_ref, wstage, bstage, xbuf, obuf, hid_ref, gsem, ssem, wsem, bsem, lsem,
                *, tm, tiles):
    del xh_in_ref
    e = pl.program_id(0)
    i = pl.program_id(1)
    s = e * tiles + i
    last = N_EXPERTS * tiles - 1
    slot = lax.rem(s, 2)
    other = 1 - slot
    base = s * tm
    mats = ((wg_hbm, wg_ref), (wu_hbm, wu_ref), (wd_hbm, wd_ref))

    def gather_copy(tok, r, sl):
        return pltpu.make_async_copy(xh_ref.at[pl.ds(tok, 1), :], xbuf.at[sl, pl.ds(r, 1), :], gsem.at[sl])

    def scatter_copy(tok, r, sl):
        return pltpu.make_async_copy(obuf.at[sl, pl.ds(r, 1), :],
                                     xh_ref.at[pl.ds(tok, 1), pl.ds(0, D_MODEL)], ssem.at[sl])

    def start_rows(copy, first, sl):
        def body(r, c):
            copy(idx_ref[first + r], r, sl).start()
            return c
        lax.fori_loop(0, tm, body, 0, unroll=8)

    def wait_rows(copy, sl):
        for _ in range(tm):
            copy(0, 0, sl).wait()

    def load_weights():
        jobs = [(src, dst, c) for src, dst in mats for c in range(W_CHUNKS)]

        def chunk_copy(k):
            src, _, c = jobs[k]
            return pltpu.make_async_copy(src.at[e, pl.ds(c * W_ROWS, W_ROWS), :], wstage.at[k % 2], wsem.at[k % 2])

        chunk_copy(0).start()
        for k, (_, dst, c) in enumerate(jobs):
            if k + 1 < len(jobs):
                chunk_copy(k + 1).start()
            chunk_copy(k).wait()
            dst[c * W_ROWS:(c + 1) * W_ROWS, :] = wstage[k % 2].astype(BF16)

    active = 1 << (min(tiles, W_CHUNKS).bit_length() - 1)
    per_step = W_CHUNKS // active
    nxt = lax.rem(e + 1, N_EXPERTS)
    conv_jobs = [(m, j) for m in range(len(mats)) for j in range(per_step)]

    def conv_row(tile, j):
        return pl.multiple_of((lax.rem(tile, active) * per_step + j) * W_ROWS, W_ROWS)

    def conv_in(m, j, tile):
        return pltpu.make_async_copy(mats[m][0].at[nxt, pl.ds(conv_row(tile, j), W_ROWS), :],
                                     wstage.at[m * per_step + j], wsem.at[0])

    def conv_out(m, j):
        return pltpu.make_async_copy(bstage.at[m * per_step + j],
                                     wbf_hbm.at[nxt, pl.ds(m * D_MODEL + conv_row(i, j), W_ROWS), :], bsem.at[0])

    load_rows = D_MODEL // W_LOAD_PARTS
    load_jobs = [(m, q) for m in range(len(mats)) for q in range(W_LOAD_PARTS)]

    def weight_load(m, q, expert):
        return pltpu.make_async_copy(wbf_hbm.at[expert, pl.ds(m * D_MODEL + q * load_rows, load_rows), :],
                                     mats[m][1].at[pl.ds(q * load_rows, load_rows), :], lsem.at[0])

    @pl.when(s >= 1)
    def _():
        wait_rows(scatter_copy, slot)

    @pl.when(i >= 1)
    def _():
        for m, j in conv_jobs:
            conv_out(m, j).wait()

    @pl.when(s == 0)
    def _():
        start_rows(gather_copy, base, slot)

    @pl.when(i == 0)
    def _():
        @pl.when(e == 0)
        def _():
            load_weights()

        @pl.when(e >= 1)
        def _():
            for m, q in load_jobs:
                weight_load(m, q, e).wait()

        for m, j in conv_jobs:
            conv_in(m, j, i).start()

    wait_rows(gather_copy, slot)
    for m, j in conv_jobs:
        conv_in(m, j, i).wait()
    for m, j in conv_jobs:
        bstage[m * per_step + j] = wstage[m * per_step + j].astype(BF16)
        conv_out(m, j).start()

    @pl.when(s == 0)
    def _():
        obuf[other] = xbuf[slot, :, 0:D_MODEL]

    wb_first = jnp.where(s == 0, base, base - tm)
    pf_first = jnp.where(s < last, base + tm, base)
    rows_per_chunk = -(-tm // (MOE_CHUNKS - MOE_QUIET_CHUNKS))
    fw = D_MODEL // MOE_CHUNKS
    xs = xbuf[slot, :, D_MODEL:2 * D_MODEL].astype(BF16)
    for c in range(MOE_CHUNKS):
        a = jnp.dot(xs, wg_ref[:, c * fw:(c + 1) * fw], preferred_element_type=F32)
        up = jnp.dot(xs, wu_ref[:, c * fw:(c + 1) * fw], preferred_element_type=F32)
        hid_ref[:, c * fw:(c + 1) * fw] = (a * jax.nn.sigmoid(a) * up).astype(BF16)
        for r in range(min(c * rows_per_chunk, tm), min((c + 1) * rows_per_chunk, tm)):
            scatter_copy(idx_ref[wb_first + r], r, other).start()
            gather_copy(idx_ref[pf_first + r], r, other).start()
    y = jnp.dot(hid_ref[...], wd_ref[...], preferred_element_type=F32)
    lane = lax.broadcasted_iota(jnp.int32, (tm, LANES), 1)
    gate = jnp.sum(jnp.where(lane == e, xbuf[slot, :, 2 * D_MODEL:], 0.0), axis=1, keepdims=True)
    obuf[slot] = xbuf[slot, :, 0:D_MODEL] + y * gate

    @pl.when(i + 1 < tiles)
    def _():
        for m, j in conv_jobs:
            conv_in(m, j, i + 1).start()

    @pl.when(i == tiles - 1)
    def _():
        for m, j in conv_jobs:
            conv_out(m, j).wait()

        @pl.when(e + 1 < N_EXPERTS)
        def _():
            for m, q in load_jobs:
                weight_load(m, q, e + 1).start()

    @pl.when(s == last)
    def _():
        wait_rows(gather_copy, other)
        wait_rows(scatter_copy, other)
        start_rows(scatter_copy, base, slot)
        wait_rows(scatter_copy, slot)


def _moe(idx_parts, xh, w_gate, w_up, w_down, tm):
    assert len(idx_parts) >= 2 and all(a.shape[1] % tm == 0 and a.shape[1] // tm >= 2 for a in idx_parts)
    idx = jnp.concatenate(idx_parts, axis=1)
    cap = idx.shape[1]
    tiles = cap // tm
    assert tm % MOE_CHUNKS == 0
    per_step = W_CHUNKS // (1 << (min(tiles, W_CHUNKS).bit_length() - 1))
    kern = functools.partial(_moe_kernel, tm=tm, tiles=tiles)
    any_spec = pl.BlockSpec(memory_space=pl.ANY)
    xh, _ = pl.pallas_call(
        kern,
        grid_spec=pltpu.PrefetchScalarGridSpec(
            num_scalar_prefetch=1,
            grid=(N_EXPERTS, tiles),
            in_specs=[any_spec, any_spec, any_spec, any_spec],
            out_specs=[any_spec, any_spec],
            scratch_shapes=[pltpu.VMEM((D_MODEL, D_MODEL), BF16),
                            pltpu.VMEM((D_MODEL, D_MODEL), BF16),
                            pltpu.VMEM((D_MODEL, D_MODEL), BF16),
                            pltpu.VMEM((max(2, 3 * per_step), W_ROWS, D_MODEL), F32),
                            pltpu.VMEM((3 * per_step, W_ROWS, D_MODEL), BF16),
                            pltpu.VMEM((2, tm, XH_WIDTH), F32),
                            pltpu.VMEM((2, tm, D_MODEL), F32),
                            pltpu.VMEM((tm, D_MODEL), BF16),
                            pltpu.SemaphoreType.DMA((2,)),
                            pltpu.SemaphoreType.DMA((2,)),
                            pltpu.SemaphoreType.DMA((2,)),
                            pltpu.SemaphoreType.DMA((1,)),
                            pltpu.SemaphoreType.DMA((1,))],
        ),
        out_shape=[jax.ShapeDtypeStruct(xh.shape, F32),
                   jax.ShapeDtypeStruct((N_EXPERTS, 3 * D_MODEL, D_MODEL), BF16)],
        input_output_aliases={1: 0},
        compiler_params=pltpu.CompilerParams(dimension_semantics=("arbitrary", "arbitrary"),
                                             vmem_limit_bytes=VMEM_LIMIT, disable_bounds_checks=True),
        name="moe_ffn",
    )(idx.reshape(-1), xh, w_gate, w_up, w_down)
    return xh


def _final_kernel(x_ref, g_ref, o_ref):
    o_ref[...] = _rms(x_ref[...], g_ref[...])


def _final_norm(xh, g_final, row0, n):
    tm = min(512, n)
    blk0 = row0 // tm
    assert row0 % tm == 0
    return pl.pallas_call(
        _final_kernel,
        grid=(n // tm,),
        in_specs=[pl.BlockSpec((tm, D_MODEL), lambda i: (blk0 + i, 0)), _const_spec((1, D_MODEL))],
        out_specs=pl.BlockSpec((tm, D_MODEL), lambda i: (i, 0)),
        out_shape=jax.ShapeDtypeStruct((n, D_MODEL), F32),
        compiler_params=_cparams(("parallel",)),
        name="final_norm",
    )(xh, g_final.reshape(1, D_MODEL))


def _forward(xs, mems, p, moe_tm=256):
    t = xs[0].shape[1]
    assert all(x.shape[1] == t for x in xs)
    sizes = [x.shape[0] * t for x in xs]
    starts = [sum(sizes[:j]) for j in range(len(xs))]
    b = sum(x.shape[0] for x in xs)
    wts = (p["w_pool"], p["pool_scale"], p["w_out"], p["g_mq"], p["w_mq"], p["w_mo"], p["g_ffn"],
           p["w_router"])
    q, k, v, u = _in_proj([x.reshape(-1, D_MODEL) for x in xs], p["g_mix"], p["w_in"])
    ona = _na_attention(q, k, v, p["na_bias"], b, t)
    mk, mv = _mem_kv(jnp.concatenate(mems, axis=0), p["g_mkv"], p["w_mk"], p["w_mv"])
    xh, aff_t = _mix(xs, ona, u, mk, mv, wts, t)
    idx = [_route(aff_t[:, row0:row0 + n], row0) for row0, n in zip(starts, sizes)]
    xh = _moe(idx, xh, p["w_gate"], p["w_up"], p["w_down"], moe_tm)
    return tuple(_final_norm(xh, p["g_final"], row0, n).reshape(x.shape)
                 for x, row0, n in zip(xs, starts, sizes))


def _prepare(g_mix, w_in, rpb, w_pool, pool_scale, w_out, g_mq, g_mkv, w_mq, w_mk, w_mv, w_mo,
             g_ffn, w_router, w_gate, w_up, w_down, g_final):
    wr = jnp.pad(w_router[0], ((0, 0), (0, LANES - N_EXPERTS)))
    wr_hi = wr.astype(BF16)
    wr_lo = (wr - wr_hi.astype(F32)).astype(BF16)
    row = lambda a: a.reshape(1, -1)
    return dict(
        g_mix=g_mix[0], w_in=w_in[0].astype(BF16), na_bias=_na_bias_table(rpb[0]),
        w_pool=w_pool[0].astype(BF16), pool_scale=row(pool_scale[0]), w_out=w_out[0].astype(BF16),
        g_mq=row(g_mq[0]), g_mkv=g_mkv[0], w_mq=w_mq[0].astype(BF16), w_mk=w_mk[0].astype(BF16),
        w_mv=w_mv[0].astype(BF16), w_mo=w_mo[0].astype(BF16), g_ffn=row(g_ffn[0]),
        w_router=jnp.concatenate([wr_hi, wr_lo], axis=1),
        w_gate=w_gate[0], w_up=w_up[0], w_down=w_down[0], g_final=g_final)


def kernel(x_prompt, x_sample, mem_prompt, mem_sample, g_mix, w_in, rpb, w_pool, pool_scale, w_out,
           g_mq, g_mkv, w_mq, w_mk, w_mv, w_mo, g_ffn, w_router, w_gate, w_up, w_down, g_final):
    p = _prepare(g_mix, w_in, rpb, w_pool, pool_scale, w_out, g_mq, g_mkv, w_mq, w_mk, w_mv, w_mo,
                 g_ffn, w_router, w_gate, w_up, w_down, g_final)
    return _forward((x_prompt, x_sample), (mem_prompt, mem_sample), p)
```

```python
import functools

import jax
import jax.numpy as jnp
from jax import lax
from jax.experimental import pallas as pl
from jax.experimental.pallas import tpu as pltpu

F32 = jnp.float32
BF16 = jnp.bfloat16

D_MODEL = 2048
GRID_W = 64
NA_WIDTH = 1024
NA_HEAD_DIM = 64
NA_HEADS = 16
WIN_ROWS = 8
WIN_COLS = 16
POOL_WINDOWS = (2, 4, 8, 16)
POOL_WIDTH = 1024
POOL_GROUP = 256
MEM_HEADS = 4
MEM_HEAD_DIM = 128
MEM_WIDTH = 512
N_EXPERTS = 16
EC_CAPACITY_FACTOR = 2
EPS = 1e-6
NEG_INF = -1e30
LOG2_E = 1.4426950408889634

LANES = 128
POOL_HALO = 16
XH_WIDTH = 2 * D_MODEL + LANES
VMEM_LIMIT = 56 * 1024 * 1024
MOE_CHUNKS = 8
W_ROWS = 128
W_CHUNKS = D_MODEL // W_ROWS
NA_ROWS = 32
NA_GROUP = 2
MIX_PARTS = 2


def _cparams(sem, vmem=VMEM_LIMIT):
    return pltpu.CompilerParams(dimension_semantics=sem, vmem_limit_bytes=vmem)


def _const_spec(shape):
    nd = len(shape)
    return pl.BlockSpec(shape, lambda *_: (0,) * nd, pipeline_mode=pl.Buffered(1))


def _rms(v, g):
    return v * lax.rsqrt(jnp.mean(v * v, axis=-1, keepdims=True) + EPS) * g


def _pick(step, starts, refs, read):
    val = read(refs[0])
    for start, ref in zip(starts[1:], refs[1:]):
        val = jnp.where(step >= start, read(ref), val)
    return val


def _held_index(step, start, count):
    return jnp.clip(step - start, 0, count - 1)


def _in_proj_kernel(*refs, starts):
    n_src = len(starts)
    x_refs, (g_ref, w_ref), outs = refs[:n_src], refs[n_src:n_src + 2], refs[n_src + 2:]
    x = _pick(pl.program_id(0), starts, x_refs, lambda r: r[...])
    h = _rms(x, g_ref[...]).astype(BF16)
    for j, o_ref in enumerate(outs):
        o_ref[...] = jnp.dot(h, w_ref[:, j * NA_WIDTH:(j + 1) * NA_WIDTH],
                             preferred_element_type=F32).astype(BF16)


def _in_proj(x2ds, g_mix, w_in_bf):
    tm = min(512, *(x.shape[0] for x in x2ds))
    counts = [x.shape[0] // tm for x in x2ds]
    starts = tuple(sum(counts[:j]) for j in range(len(x2ds)))
    n = sum(x.shape[0] for x in x2ds)
    out = jax.ShapeDtypeStruct((n, NA_WIDTH), BF16)
    row = lambda i: (i, 0)
    x_specs = [pl.BlockSpec((tm, D_MODEL), functools.partial(lambda i, s, c: (_held_index(i, s, c), 0), s=s, c=c))
               for s, c in zip(starts, counts)]
    return pl.pallas_call(
        functools.partial(_in_proj_kernel, starts=starts),
        grid=(n // tm,),
        in_specs=x_specs + [_const_spec((1, D_MODEL)), _const_spec((D_MODEL, 4 * NA_WIDTH))],
        out_specs=[pl.BlockSpec((tm, NA_WIDTH), row)] * 4,
        out_shape=[out] * 4,
        compiler_params=_cparams(("parallel",)),
        name="in_proj",
    )(*x2ds, g_mix.reshape(1, D_MODEL), w_in_bf)


def _na_kernel(q_ref, k_ref, v_ref, b_ref, o_ref, *, rows_per_step, n_rows):
    rb = pl.program_id(2)
    lane = lax.broadcasted_iota(jnp.int32, (1, LANES), 1)
    m0 = jnp.where(lane < NA_HEAD_DIM, 1.0, 0.0).astype(BF16)
    m1 = jnp.where(lane >= NA_HEAD_DIM, 1.0, 0.0).astype(BF16)
    lane_f = lax.broadcasted_iota(jnp.int32, (GRID_W, LANES), 1)
    scale = NA_HEAD_DIM ** -0.5 * LOG2_E

    def score_stage(g):
        out = []
        for j in range(NA_GROUP):
            i = g * NA_GROUP + j
            r = rb * rows_per_step + i
            r0 = jnp.clip(r - WIN_ROWS // 2, 0, n_rows - WIN_ROWS)
            qr = q_ref[0, i * GRID_W:(i + 1) * GRID_W, :]
            qq = jnp.concatenate([qr * m0, qr * m1], axis=0)
            kstart = pl.multiple_of(r0 * GRID_W, GRID_W)
            kw = k_ref[0, pl.ds(kstart, WIN_ROWS * GRID_W), :]
            s = lax.dot_general(qq, kw, (((1,), (1,)), ((), ())), preferred_element_type=F32)
            out.append((i, r - r0, kstart, s))
        return out

    def softmax_stage(rows):
        out = []
        for i, d, kstart, s in rows:
            s = s * scale + b_ref[0, d]
            p = jnp.exp2(s - jnp.max(s, axis=-1, keepdims=True))
            out.append((i, kstart, p.astype(BF16), jnp.sum(p, axis=-1, keepdims=True)))
        return out

    def value_stage(rows):
        for i, kstart, p, l in rows:
            vw = v_ref[0, pl.ds(kstart, WIN_ROWS * GRID_W), :]
            o = jnp.dot(p, vw, preferred_element_type=F32) / l
            out = jnp.where(lane_f < NA_HEAD_DIM, o[:GRID_W], o[GRID_W:])
            o_ref[0, i * GRID_W:(i + 1) * GRID_W, :] = out.astype(BF16)

    n_groups = rows_per_step // NA_GROUP
    scored, soft = None, None
    for g in range(n_groups + 2):
        nxt_scored = score_stage(g) if g < n_groups else None
        nxt_soft = softmax_stage(scored) if scored is not None else None
        if soft is not None:
            value_stage(soft)
        scored, soft = nxt_scored, nxt_soft


def _na_bias_table(rpb):
    cols = jnp.arange(GRID_W)
    c0 = jnp.clip(cols - WIN_COLS // 2, 0, GRID_W - WIN_COLS)
    col_mask = (cols[None, :] >= c0[:, None]) & (cols[None, :] < c0[:, None] + WIN_COLS)
    dc_idx = jnp.clip(cols[None, :] - cols[:, None], -(WIN_COLS - 1), WIN_COLS - 1) + WIN_COLS - 1
    onehot = (dc_idx[:, :, None] == jnp.arange(2 * WIN_COLS - 1)).astype(F32)
    t = jnp.einsum('hrc,qkc->hqrk', rpb.astype(F32) * LOG2_E, onehot, precision=lax.Precision.HIGHEST)
    t = jnp.where(col_mask[None, :, None, :], t, NEG_INF)
    per_d = [t[:, :, WIN_ROWS - 1 - d:2 * WIN_ROWS - 1 - d, :]
             .reshape(NA_HEADS // 2, 2 * GRID_W, WIN_ROWS * GRID_W) for d in range(WIN_ROWS)]
    return jnp.stack(per_d, axis=1)


def _na_attention(q, k, v, bias, b, t):
    n_rows = t // GRID_W
    assert n_rows >= WIN_ROWS
    rows_per_step = min(NA_ROWS, n_rows)
    assert rows_per_step % NA_GROUP == 0 and n_rows % rows_per_step == 0
    tq = rows_per_step * GRID_W
    q3, k3, v3 = (a.reshape(b, t, NA_WIDTH) for a in (q, k, v))
    n_pairs = NA_HEADS // 2
    kern = functools.partial(_na_kernel, rows_per_step=rows_per_step, n_rows=n_rows)
    out = pl.pallas_call(
        kern,
        grid=(b, n_pairs, n_rows // rows_per_step),
        in_specs=[pl.BlockSpec((1, tq, LANES), lambda bi, p, r: (bi, r, p)),
                  pl.BlockSpec((1, t, LANES), lambda bi, p, r: (bi, 0, p)),
                  pl.BlockSpec((1, t, LANES), lambda bi, p, r: (bi, 0, p)),
                  pl.BlockSpec((1, WIN_ROWS, 2 * GRID_W, WIN_ROWS * GRID_W),
                               lambda bi, p, r: (p, 0, 0, 0))],
        out_specs=pl.BlockSpec((1, tq, LANES), lambda bi, p, r: (bi, r, p)),
        out_shape=jax.ShapeDtypeStruct((b, t, NA_WIDTH), BF16),
        compiler_params=_cparams(("parallel", "parallel", "parallel")),
        name="na_attention",
    )(q3, k3, v3, bias)
    return out.reshape(b * t, NA_WIDTH)


def _mem_kv_kernel(m_ref, g_ref, wk_ref, wv_ref, k_ref, v_ref):
    h = _rms(m_ref[0], g_ref[...]).astype(BF16)
    k_ref[0] = jnp.dot(h, wk_ref[...], preferred_element_type=F32).astype(BF16)
    v_ref[0] = jnp.dot(h, wv_ref[...], preferred_element_type=F32).astype(BF16)


def _mem_kv(mem, g_mkv, w_mk_bf, w_mv_bf):
    b, m, _ = mem.shape
    out = jax.ShapeDtypeStruct((b, m, MEM_WIDTH), BF16)
    return pl.pallas_call(
        _mem_kv_kernel,
        grid=(b,),
        in_specs=[pl.BlockSpec((1, m, D_MODEL), lambda i: (i, 0, 0)),
                  _const_spec((1, D_MODEL)),
                  _const_spec((D_MODEL, MEM_WIDTH)),
                  _const_spec((D_MODEL, MEM_WIDTH))],
        out_specs=[pl.BlockSpec((1, m, MEM_WIDTH), lambda i: (i, 0, 0))] * 2,
        out_shape=[out] * 2,
        compiler_params=_cparams(("parallel",)),
        name="mem_kv",
    )(mem, g_mkv.reshape(1, D_MODEL), w_mk_bf, w_mv_bf)


def _mix_kernel(*refs, tm, seq, batch_starts):
    x_refs = refs[:len(batch_starts)]
    (ona_ref, up_ref, uc_ref, un_ref, mk_ref, mv_ref, wpool_ref, pscale_ref, wout_ref, gmq_ref,
     wmq_ref, wmo_ref, gffn_ref, wr_ref, xh_ref, afft_ref) = refs[len(batch_starts):]
    t0 = pl.program_id(1) * tm
    pm = tm // MIX_PARTS
    width = pm + 2 * POOL_HALO
    ucat = jnp.concatenate([up_ref[0], uc_ref[0], un_ref[0]], axis=0)
    mk = mk_ref[0]
    mv = mv_ref[0]
    lane = lax.broadcasted_iota(jnp.int32, (pm, LANES), 1)
    parts = [slice(j * pm, (j + 1) * pm) for j in range(MIX_PARTS)]

    def pool(rows):
        p0 = t0 + rows.start
        useg = ucat[rows.start:rows.start + width]
        pos = p0 + lax.broadcasted_iota(jnp.int32, (pm, width), 0)
        src = p0 - POOL_HALO + lax.broadcasted_iota(jnp.int32, (pm, width), 1)
        pcol = p0 + lax.broadcasted_iota(jnp.int32, (pm, 1), 0)
        pools = []
        for g, w in enumerate(POOL_WINDOWS):
            sl = slice(g * POOL_GROUP, (g + 1) * POOL_GROUP)
            lo = jnp.maximum(pos - w // 2, 0)
            hi = jnp.minimum(pos + w // 2, seq)
            band = jnp.where(src >= lo, jnp.where(src < hi, 1.0, 0.0), 0.0).astype(BF16)
            wsum = jnp.dot(band, useg[:, sl], preferred_element_type=F32)
            cnt = (jnp.minimum(pcol + w // 2, seq) - jnp.maximum(pcol - w // 2, 0)).astype(F32)
            dlt = (wsum / cnt - useg[POOL_HALO:POOL_HALO + pm, sl].astype(F32)).astype(BF16)
            pools.append(jnp.dot(dlt, wpool_ref[g], preferred_element_type=F32))
        return (jnp.concatenate(pools, axis=1) * pscale_ref[...]).astype(BF16)

    def mix(rows, o_pool):
        x = _pick(pl.program_id(0), batch_starts, x_refs, lambda r: r[0, rows, :])
        return (x
                + jnp.dot(ona_ref[0, rows, :], wout_ref[:NA_WIDTH, :], preferred_element_type=F32)
                + jnp.dot(o_pool, wout_ref[NA_WIDTH:, :], preferred_element_type=F32))

    def mem_query(x1):
        hq = _rms(x1, gmq_ref[...]).astype(BF16)
        return jnp.dot(hq, wmq_ref[...], preferred_element_type=F32).astype(BF16)

    def mem_attend(q):
        heads = []
        for h in range(MEM_HEADS):
            sl = slice(h * MEM_HEAD_DIM, (h + 1) * MEM_HEAD_DIM)
            s = lax.dot_general(q[:, sl], mk[:, sl], (((1,), (1,)), ((), ())),
                                preferred_element_type=F32) * (MEM_HEAD_DIM ** -0.5)
            p = jnp.exp(s - jnp.max(s, axis=-1, keepdims=True))
            l = jnp.sum(p, axis=-1, keepdims=True)
            heads.append(jnp.dot(p.astype(BF16), mv[:, sl], preferred_element_type=F32) / l)
        return jnp.concatenate(heads, axis=1).astype(BF16)

    def mem_out(x1, o_mem):
        return x1 + jnp.dot(o_mem, wmo_ref[...], preferred_element_type=F32)

    def route(rows, x2):
        h3 = _rms(x2, gffn_ref[...])
        h_hi = h3.astype(BF16)
        h_lo = (h3 - h_hi.astype(F32)).astype(BF16)
        both = jnp.dot(h_hi, wr_ref[...], preferred_element_type=F32)
        logits = (both[:, :LANES] + both[:, LANES:]
                  + jnp.dot(h_lo, wr_ref[:, :LANES], preferred_element_type=F32))
        logits = jnp.where(lane < N_EXPERTS, logits, NEG_INF)
        ex = jnp.exp(logits - jnp.max(logits, axis=-1, keepdims=True))
        aff = ex / jnp.sum(ex, axis=-1, keepdims=True)
        xh_ref[rows, 0:D_MODEL] = x2
        xh_ref[rows, D_MODEL:2 * D_MODEL] = h3
        xh_ref[rows, 2 * D_MODEL:] = aff
        afft_ref[:, rows] = aff.T[:N_EXPERTS, :]

    o_pool = [pool(r) for r in parts]
    x1 = [mix(r, o) for r, o in zip(parts, o_pool)]
    q = [mem_query(v) for v in x1]
    o_mem = [mem_attend(v) for v in q]
    x2 = [mem_out(a, o) for a, o in zip(x1, o_mem)]
    for r, v in zip(parts, x2):
        route(r, v)


def _mix(xs, ona, u, mk, mv, wts, t):
    tm = min(256, t)
    nt = t // tm
    hb = tm // POOL_HALO
    last_hb = t // POOL_HALO - 1
    batches = [x.shape[0] for x in xs]
    batch_starts = tuple(sum(batches[:j]) for j in range(len(xs)))
    b = sum(batches)
    n = b * t
    ona3 = ona.reshape(b, t, NA_WIDTH)
    u3 = u.reshape(b, t, POOL_WIDTH)

    def x_spec(start, count):
        def index(bi, i):
            tile = jnp.where(bi < start, 0, jnp.where(bi >= start + count, nt - 1, i))
            return (_held_index(bi, start, count), tile, 0)
        return pl.BlockSpec((1, tm, D_MODEL), index)

    kern = functools.partial(_mix_kernel, tm=tm, seq=t, batch_starts=batch_starts)
    return pl.pallas_call(
        kern,
        grid=(b, nt),
        in_specs=[x_spec(s, c) for s, c in zip(batch_starts, batches)] + [
            pl.BlockSpec((1, tm, NA_WIDTH), lambda bi, i: (bi, i, 0)),
            pl.BlockSpec((1, POOL_HALO, POOL_WIDTH), lambda bi, i: (bi, jnp.maximum(i * hb - 1, 0), 0)),
            pl.BlockSpec((1, tm, POOL_WIDTH), lambda bi, i: (bi, i, 0)),
            pl.BlockSpec((1, POOL_HALO, POOL_WIDTH),
                         lambda bi, i: (bi, jnp.minimum((i + 1) * hb, last_hb), 0)),
            pl.BlockSpec((1, mk.shape[1], MEM_WIDTH), lambda bi, i: (bi, 0, 0)),
            pl.BlockSpec((1, mk.shape[1], MEM_WIDTH), lambda bi, i: (bi, 0, 0)),
            _const_spec((len(POOL_WINDOWS), POOL_GROUP, POOL_GROUP)),
            _const_spec((1, POOL_WIDTH)),
            _const_spec((D_MODEL, D_MODEL)),
            _const_spec((1, D_MODEL)),
            _const_spec((D_MODEL, MEM_WIDTH)),
            _const_spec((MEM_WIDTH, D_MODEL)),
            _const_spec((1, D_MODEL)),
            _const_spec((D_MODEL, 2 * LANES)),
        ],
        out_specs=[pl.BlockSpec((tm, XH_WIDTH), lambda bi, i: (bi * nt + i, 0)),
                   pl.BlockSpec((N_EXPERTS, tm), lambda bi, i: (0, bi * nt + i))],
        out_shape=[jax.ShapeDtypeStruct((n, XH_WIDTH), F32),
                   jax.ShapeDtypeStruct((N_EXPERTS, n), F32)],
        compiler_params=_cparams(("parallel", "parallel")),
        name="mix_mem_router",
    )(*xs, ona3, u3, u3, u3, mk, mv, *wts)


def _total(v):
    return jnp.sum(jnp.sum(v, axis=1, keepdims=True), axis=0, keepdims=True)


def _bits_as_float(bits):
    return pltpu.bitcast(bits, F32)[0:1, 0:1]


def _threshold_kernel(a_ref, thr_ref, *, cap):
    capf = jnp.float32(cap)

    def search(b, los):
        bit = jnp.left_shift(jnp.int32(1), 30 - b)
        out = []
        for e in range(N_EXPERTS):
            cand = los[e] | bit
            cnt = _total(jnp.where(a_ref[e] >= _bits_as_float(cand), 1.0, 0.0))
            out.append(jnp.where(cnt >= capf, cand, los[e]))
        return tuple(out)

    los = lax.fori_loop(0, 31, search, tuple(jnp.zeros((8, LANES), jnp.int32) for _ in range(N_EXPERTS)))
    for e in range(N_EXPERTS):
        thr_ref[e] = los[e]


def _route_kernel(a_ref, thr_ref, idx_ref, *, nch, cap):
    a = a_ref[0]
    capf = jnp.float32(cap)
    total, as_float = _total, _bits_as_float
    thr_bits = thr_ref[0]
    thr = as_float(thr_bits)
    gt = a > thr
    eq = jnp.logical_and(a >= thr, jnp.logical_not(gt))
    need = capf - total(jnp.where(gt, 1.0, 0.0))

    kr = lax.broadcasted_iota(jnp.int32, (LANES, LANES), 0)
    kc = lax.broadcasted_iota(jnp.int32, (LANES, LANES), 1)
    tri_lane = jnp.where(kr <= kc, 1.0, 0.0).astype(BF16)
    cr = lax.broadcasted_iota(jnp.int32, (nch, nch), 0)
    cc = lax.broadcasted_iota(jnp.int32, (nch, nch), 1)
    tri_excl = jnp.where(cc < cr, 1.0, 0.0).astype(BF16)
    tri_incl = jnp.where(cc <= cr, 1.0, 0.0).astype(BF16)

    def prefix(maskf, chunk_tri):
        within = jnp.dot(maskf.astype(BF16), tri_lane, preferred_element_type=F32)
        tot = jnp.broadcast_to(within[:, LANES - 1:LANES], (nch, LANES))
        before = jnp.dot(chunk_tri, tot.astype(BF16), preferred_element_type=F32)
        return within, before

    eqf = jnp.where(eq, 1.0, 0.0)
    eq_within, eq_before = prefix(eqf, tri_excl)
    eq_rank = eq_before + eq_within - eqf
    self_ = jnp.where(gt, 1.0, jnp.where(eq, jnp.where(eq_rank < need, 1.0, 0.0), 0.0))

    within, cs_incl = prefix(self_, tri_incl)
    cs_col = cs_incl[:, 0:1]
    j = lax.broadcasted_iota(jnp.int32, (1, cap), 1).astype(F32)
    done = cs_col <= j
    chunk = jnp.sum(jnp.where(done, 1.0, 0.0), axis=0, keepdims=True)
    cs_prev = jnp.max(jnp.where(done, cs_col, 0.0), axis=0, keepdims=True)
    crow = lax.broadcasted_iota(jnp.int32, (nch, cap), 0).astype(F32)
    onehot = jnp.where(crow == chunk, 1.0, 0.0).astype(BF16)
    wsel = jnp.dot(within.T.astype(BF16), onehot, preferred_element_type=F32)
    rank = j - cs_prev
    inner = jnp.sum(jnp.where(wsel <= rank, 1.0, 0.0), axis=0, keepdims=True)
    idx_ref[0] = (chunk * LANES + inner).astype(jnp.int32)


def _route(aff_t, row0):
    n = aff_t.shape[1]
    cap = EC_CAPACITY_FACTOR * n // N_EXPERTS
    nch = n // LANES
    aff3 = aff_t.reshape(N_EXPERTS, nch, LANES)
    thr = pl.pallas_call(
        functools.partial(_threshold_kernel, cap=cap),
        grid=(1,),
        in_specs=[pl.BlockSpec((N_EXPERTS, nch, LANES), lambda i: (0, 0, 0))],
        out_specs=pl.BlockSpec((N_EXPERTS, 8, LANES), lambda i: (0, 0, 0)),
        out_shape=jax.ShapeDtypeStruct((N_EXPERTS, 8, LANES), jnp.int32),
        compiler_params=_cparams(("arbitrary",)),
        name="route_threshold",
    )(aff3)
    kern = functools.partial(_route_kernel, nch=nch, cap=cap)
    idx = pl.pallas_call(
        kern,
        grid=(N_EXPERTS,),
        in_specs=[pl.BlockSpec((1, nch, LANES), lambda e: (e, 0, 0)),
                  pl.BlockSpec((1, 8, LANES), lambda e: (e, 0, 0))],
        out_specs=pl.BlockSpec((1, 1, cap), lambda e: (e, 0, 0)),
        out_shape=jax.ShapeDtypeStruct((N_EXPERTS, 1, cap), jnp.int32),
        compiler_params=_cparams(("parallel",)),
        name="route",
    )(aff3, thr)
    return idx.reshape(N_EXPERTS, cap) + row0


def _moe_kernel(idx_ref, xh_in_ref, wg_hbm, wu_hbm, wd_hbm, xh_ref, wbf_hbm,
                wg_ref, wu_ref, wd_ref, wstage, bstage, xbuf, obuf, hid_ref, gsem, ssem, wsem, bsem, lsem,
                *, tm, tiles):
    del xh_in_ref
    e = pl.program_id(0)
    i = pl.program_id(1)
    s = e * tiles + i
    last = N_EXPERTS * tiles - 1
    slot = lax.rem(s, 2)
    other = 1 - slot
    base = s * tm
    mats = ((wg_hbm, wg_ref), (wu_hbm, wu_ref), (wd_hbm, wd_ref))

    def gather_copy(tok, r, sl):
        return pltpu.make_async_copy(xh_ref.at[pl.ds(tok, 1), :], xbuf.at[sl, pl.ds(r, 1), :], gsem.at[sl])

    def scatter_copy(tok, r, sl):
        return pltpu.make_async_copy(obuf.at[sl, pl.ds(r, 1), :],
                                     xh_ref.at[pl.ds(tok, 1), pl.ds(0, D_MODEL)], ssem.at[sl])

    def start_rows(copy, first, sl):
        def body(r, c):
            copy(idx_ref[first + r], r, sl).start()
            return c
        lax.fori_loop(0, tm, body, 0, unroll=8)

    def wait_rows(copy, sl):
        for _ in range(tm):
            copy(0, 0, sl).wait()

    def load_weights():
        jobs = [(src, dst, c) for src, dst in mats for c in range(W_CHUNKS)]

        def chunk_copy(k):
            src, _, c = jobs[k]
            return pltpu.make_async_copy(src.at[e, pl.ds(c * W_ROWS, W_ROWS), :], wstage.at[k % 2], wsem.at[k % 2])

        chunk_copy(0).start()
        for k, (_, dst, c) in enumerate(jobs):
            if k + 1 < len(jobs):
                chunk_copy(k + 1).start()
            chunk_copy(k).wait()
            dst[c * W_ROWS:(c + 1) * W_ROWS, :] = wstage[k % 2].astype(BF16)

    active = 1 << (min(tiles, W_CHUNKS).bit_length() - 1)
    per_step = W_CHUNKS // active
    nxt = lax.rem(e + 1, N_EXPERTS)
    conv_jobs = [(m, j) for m in range(len(mats)) for j in range(per_step)]

    def conv_row(tile, j):
        return pl.multiple_of((lax.rem(tile, active) * per_step + j) * W_ROWS, W_ROWS)

    def conv_in(m, j, tile):
        return pltpu.make_async_copy(mats[m][0].at[nxt, pl.ds(conv_row(tile, j), W_ROWS), :],
                                     wstage.at[m * per_step + j], wsem.at[0])

    def conv_out(m, j):
        return pltpu.make_async_copy(bstage.at[m * per_step + j],
                                     wbf_hbm.at[nxt, pl.ds(m * D_MODEL + conv_row(i, j), W_ROWS), :], bsem.at[0])

    def weight_load(m, expert):
        return pltpu.make_async_copy(wbf_hbm.at[expert, pl.ds(m * D_MODEL, D_MODEL), :], mats[m][1], lsem.at[0])

    @pl.when(s >= 1)
    def _():
        wait_rows(scatter_copy, slot)

    @pl.when(i >= 1)
    def _():
        for m, j in conv_jobs:
            conv_out(m, j).wait()

    @pl.when(s == 0)
    def _():
        start_rows(gather_copy, base, slot)

    @pl.when(i == 0)
    def _():
        @pl.when(e == 0)
        def _():
            load_weights()

        @pl.when(e >= 1)
        def _():
            for m in range(len(mats)):
                weight_load(m, e).wait()

        for m, j in conv_jobs:
            conv_in(m, j, i).start()

    wait_rows(gather_copy, slot)
    for m, j in conv_jobs:
        conv_in(m, j, i).wait()
    for m, j in conv_jobs:
        bstage[m * per_step + j] = wstage[m * per_step + j].astype(BF16)
        conv_out(m, j).start()

    @pl.when(s == 0)
    def _():
        obuf[other] = xbuf[slot, :, 0:D_MODEL]

    wb_first = jnp.where(s == 0, base, base - tm)
    pf_first = jnp.where(s < last, base + tm, base)
    rows_per_chunk = tm // MOE_CHUNKS
    fw = D_MODEL // MOE_CHUNKS
    xs = xbuf[slot, :, D_MODEL:2 * D_MODEL].astype(BF16)
    for c in range(MOE_CHUNKS):
        a = jnp.dot(xs, wg_ref[:, c * fw:(c + 1) * fw], preferred_element_type=F32)
        up = jnp.dot(xs, wu_ref[:, c * fw:(c + 1) * fw], preferred_element_type=F32)
        hid_ref[:, c * fw:(c + 1) * fw] = (a * jax.nn.sigmoid(a) * up).astype(BF16)
        for r in range(c * rows_per_chunk, (c + 1) * rows_per_chunk):
            scatter_copy(idx_ref[wb_first + r], r, other).start()
            gather_copy(idx_ref[pf_first + r], r, other).start()
    y = jnp.dot(hid_ref[...], wd_ref[...], preferred_element_type=F32)
    lane = lax.broadcasted_iota(jnp.int32, (tm, LANES), 1)
    gate = jnp.sum(jnp.where(lane == e, xbuf[slot, :, 2 * D_MODEL:], 0.0), axis=1, keepdims=True)
    obuf[slot] = xbuf[slot, :, 0:D_MODEL] + y * gate

    @pl.when(i + 1 < tiles)
    def _():
        for m, j in conv_jobs:
            conv_in(m, j, i + 1).start()

    @pl.when(i == tiles - 1)
    def _():
        for m, j in conv_jobs:
            conv_out(m, j).wait()

        @pl.when(e + 1 < N_EXPERTS)
        def _():
            for m in range(len(mats)):
                weight_load(m, e + 1).start()

    @pl.when(s == last)
    def _():
        wait_rows(gather_copy, other)
        wait_rows(scatter_copy, other)
        start_rows(scatter_copy, base, slot)
        wait_rows(scatter_copy, slot)


def _moe(idx_parts, xh, w_gate, w_up, w_down, tm):
    assert len(idx_parts) >= 2 and all(a.shape[1] % tm == 0 and a.shape[1] // tm >= 2 for a in idx_parts)
    idx = jnp.concatenate(idx_parts, axis=1)
    cap = idx.shape[1]
    tiles = cap // tm
    assert tm % MOE_CHUNKS == 0
    per_step = W_CHUNKS // (1 << (min(tiles, W_CHUNKS).bit_length() - 1))
    kern = functools.partial(_moe_kernel, tm=tm, tiles=tiles)
    any_spec = pl.BlockSpec(memory_space=pl.ANY)
    xh, _ = pl.pallas_call(
        kern,
        grid_spec=pltpu.PrefetchScalarGridSpec(
            num_scalar_prefetch=1,
            grid=(N_EXPERTS, tiles),
            in_specs=[any_spec, any_spec, any_spec, any_spec],
            out_specs=[any_spec, any_spec],
            scratch_shapes=[pltpu.VMEM((D_MODEL, D_MODEL), BF16),
                            pltpu.VMEM((D_MODEL, D_MODEL), BF16),
                            pltpu.VMEM((D_MODEL, D_MODEL), BF16),
                            pltpu.VMEM((max(2, 3 * per_step), W_ROWS, D_MODEL), F32),
                            pltpu.VMEM((3 * per_step, W_ROWS, D_MODEL), BF16),
                            pltpu.VMEM((2, tm, XH_WIDTH), F32),
                            pltpu.VMEM((2, tm, D_MODEL), F32),
                            pltpu.VMEM((tm, D_MODEL), BF16),
                            pltpu.SemaphoreType.DMA((2,)),
                            pltpu.SemaphoreType.DMA((2,)),
                            pltpu.SemaphoreType.DMA((2,)),
                            pltpu.SemaphoreType.DMA((1,)),
                            pltpu.SemaphoreType.DMA((1,))],
        ),
        out_shape=[jax.ShapeDtypeStruct(xh.shape, F32),
                   jax.ShapeDtypeStruct((N_EXPERTS, 3 * D_MODEL, D_MODEL), BF16)],
        input_output_aliases={1: 0},
        compiler_params=pltpu.CompilerParams(dimension_semantics=("arbitrary", "arbitrary"),
                                             vmem_limit_bytes=VMEM_LIMIT, disable_bounds_checks=True),
        name="moe_ffn",
    )(idx.reshape(-1), xh, w_gate, w_up, w_down)
    return xh


def _final_kernel(x_ref, g_ref, o_ref):
    o_ref[...] = _rms(x_ref[...], g_ref[...])


def _final_norm(xh, g_final, row0, n):
    tm = min(512, n)
    blk0 = row0 // tm
    assert row0 % tm == 0
    return pl.pallas_call(
        _final_kernel,
        grid=(n // tm,),
        in_specs=[pl.BlockSpec((tm, D_MODEL), lambda i: (blk0 + i, 0)), _const_spec((1, D_MODEL))],
        out_specs=pl.BlockSpec((tm, D_MODEL), lambda i: (i, 0)),
        out_shape=jax.ShapeDtypeStruct((n, D_MODEL), F32),
        compiler_params=_cparams(("parallel",)),
        name="final_norm",
    )(xh, g_final.reshape(1, D_MODEL))


def _forward(xs, mems, p, moe_tm=256):
    t = xs[0].shape[1]
    assert all(x.shape[1] == t for x in xs)
    sizes = [x.shape[0] * t for x in xs]
    starts = [sum(sizes[:j]) for j in range(len(xs))]
    b = sum(x.shape[0] for x in xs)
    wts = (p["w_pool"], p["pool_scale"], p["w_out"], p["g_mq"], p["w_mq"], p["w_mo"], p["g_ffn"],
           p["w_router"])
    q, k, v, u = _in_proj([x.reshape(-1, D_MODEL) for x in xs], p["g_mix"], p["w_in"])
    ona = _na_attention(q, k, v, p["na_bias"], b, t)
    mk, mv = _mem_kv(jnp.concatenate(mems, axis=0), p["g_mkv"], p["w_mk"], p["w_mv"])
    xh, aff_t = _mix(xs, ona, u, mk, mv, wts, t)
    idx = [_route(aff_t[:, row0:row0 + n], row0) for row0, n in zip(starts, sizes)]
    xh = _moe(idx, xh, p["w_gate"], p["w_up"], p["w_down"], moe_tm)
    return tuple(_final_norm(xh, p["g_final"], row0, n).reshape(x.shape)
                 for x, row0, n in zip(xs, starts, sizes))


def _prepare(g_mix, w_in, rpb, w_pool, pool_scale, w_out, g_mq, g_mkv, w_mq, w_mk, w_mv, w_mo,
             g_ffn, w_router, w_gate, w_up, w_down, g_final):
    wr = jnp.pad(w_router[0], ((0, 0), (0, LANES - N_EXPERTS)))
    wr_hi = wr.astype(BF16)
    wr_lo = (wr - wr_hi.astype(F32)).astype(BF16)
    row = lambda a: a.reshape(1, -1)
    return dict(
        g_mix=g_mix[0], w_in=w_in[0].astype(BF16), na_bias=_na_bias_table(rpb[0]),
        w_pool=w_pool[0].astype(BF16), pool_scale=row(pool_scale[0]), w_out=w_out[0].astype(BF16),
        g_mq=row(g_mq[0]), g_mkv=g_mkv[0], w_mq=w_mq[0].astype(BF16), w_mk=w_mk[0].astype(BF16),
        w_mv=w_mv[0].astype(BF16), w_mo=w_mo[0].astype(BF16), g_ffn=row(g_ffn[0]),
        w_router=jnp.concatenate([wr_hi, wr_lo], axis=1),
        w_gate=w_gate[0], w_up=w_up[0], w_down=w_down[0], g_final=g_final)


def kernel(x_prompt, x_sample, mem_prompt, mem_sample, g_mix, w_in, rpb, w_pool, pool_scale, w_out,
           g_mq, g_mkv, w_mq, w_mk, w_mv, w_mo, g_ffn, w_router, w_gate, w_up, w_down, g_final):
    p = _prepare(g_mix, w_in, rpb, w_pool, pool_scale, w_out, g_mq, g_mkv, w_mq, w_mk, w_mv, w_mo,
                 g_ffn, w_router, w_gate, w_up, w_down, g_final)
    return _forward((x_prompt, x_sample), (mem_prompt, mem_sample), p)
```
